```python
import jax, jax.numpy as jnp
from jax import lax
import numpy as np

D_MODEL = 1024
BATCH = 16
SEQ = 2048
DEPTH = 1

N_META = 16
EPS = 1e-6
SSD_HEADS = 16
SSD_HEAD_DIM = 64
SSD_INNER = SSD_HEADS * SSD_HEAD_DIM
SSD_GROUPS = 4
SSD_HPG = SSD_HEADS // SSD_GROUPS
SSD_STATE = 128
SSD_CONV = 4
SSD_CHUNK = 128
SSD_CONV_CH = SSD_INNER + 2 * SSD_GROUPS * SSD_STATE
HG_WIDTH = 1024
HG_EXPAND = 128
HG_HEADS = HG_WIDTH // HG_EXPAND
HG_HEAD_I = HG_WIDTH // HG_HEADS
HG_CHUNK = 16
D_FF = 2816
IN_SIZES = (SSD_INNER, SSD_CONV_CH, SSD_HEADS, HG_WIDTH, HG_WIDTH, HG_WIDTH, HG_WIDTH, D_MODEL, D_MODEL)
IN_TOTAL = sum(IN_SIZES)

kernel_name = "hybrid_ssd_hgrn2_macaron_block"


def _split_points():
    pts, acc = [], 0
    for s in IN_SIZES[:-1]:
        acc += s
        pts.append(acc)
    return pts


def rmsnorm(x, w):
    xf = x.astype(jnp.float32)
    y = xf * lax.rsqrt(jnp.mean(xf * xf, axis=-1, keepdims=True) + EPS)
    return (y * w.astype(jnp.float32)).astype(x.dtype)


def swiglu(x, w_gu, w_down):
    g, u = jnp.split(x @ w_gu, 2, axis=-1)
    return (jax.nn.silu(g) * u) @ w_down


def causal_depthwise_conv(x, w, b):
    y = lax.conv_general_dilated(x, w[:, None, :], window_strides=(1,), padding=[(w.shape[0] - 1, 0)],
                                 dimension_numbers=("NWC", "WIO", "NWC"), feature_group_count=x.shape[-1])
    return y + b


def segsum_exp(a):
    T = a.shape[-1]
    cs = jnp.cumsum(a, axis=-1)
    mask = jnp.tril(jnp.ones((T, T), dtype=bool))
    return jnp.exp(jnp.where(mask, cs[..., :, None] - cs[..., None, :], -jnp.inf))


def ssd_mixer(z, xbc, dt_raw, conv_w, conv_b, dt_bias, a_log, d_skip, norm_w):
    f32 = jnp.float32
    Bsz, L, _ = z.shape
    G, R, P, N, Q = SSD_GROUPS, SSD_HPG, SSD_HEAD_DIM, SSD_STATE, SSD_CHUNK
    xbc = jax.nn.silu(causal_depthwise_conv(xbc, conv_w, conv_b)).astype(f32)
    xs, Bm, Cm = jnp.split(xbc, [SSD_INNER, SSD_INNER + G * N], axis=-1)
    dt = jax.nn.softplus(dt_raw.astype(f32) + dt_bias.astype(f32))
    A = -jnp.exp(a_log.astype(f32))
    pad = (-L) % Q
    padf = lambda t: jnp.pad(t, ((0, 0), (pad, 0)) + ((0, 0),) * (t.ndim - 2))
    Lp = L + pad
    nc = Lp // Q
    x4 = padf(xs).reshape(Bsz, nc, Q, G, R, P)
    dtp = padf(dt).reshape(Bsz, nc, Q, SSD_HEADS)
    Bc = padf(Bm).reshape(Bsz, nc, Q, G, N)
    Cc = padf(Cm).reshape(Bsz, nc, Q, G, N)
    X = x4 * dtp.reshape(Bsz, nc, Q, G, R)[..., None]
    a = (dtp * A).transpose(0, 3, 1, 2)
    a_cs = jnp.cumsum(a, axis=-1)
    Lmat = segsum_exp(a).reshape(Bsz, G, R, nc, Q, Q)
    CB = jnp.einsum("bclgn,bcsgn->bcgls", Cc, Bc)
    y_diag = jnp.einsum("bcgls,bgrcls,bcsgrp->bclgrp", CB, Lmat, X)
    decay_states = jnp.exp(a_cs[..., -1:] - a_cs).reshape(Bsz, G, R, nc, Q)
    states = jnp.einsum("bclgn,bgrcl,bclgrp->cbgrpn", Bc, decay_states, X)
    chunk_decay = jnp.moveaxis(jnp.exp(a_cs[..., -1]).reshape(Bsz, G, R, nc), -1, 0)

    def step(hs, inp):
        s, dec = inp
        return hs * dec[..., None, None] + s, hs

    _, prev = lax.scan(step, jnp.zeros((Bsz, G, R, P, N), f32), (states, chunk_decay))
    y_off = jnp.einsum("bclgn,cbgrpn,bgrcl->bclgrp", Cc, prev, jnp.exp(a_cs).reshape(Bsz, G, R, nc, Q))
    y = y_diag + y_off + x4 * d_skip.astype(f32).reshape(G, R)[:, :, None]
    y = y.reshape(Bsz, Lp, SSD_INNER)[:, pad:]
    yg = (y * jax.nn.silu(z.astype(f32))).reshape(Bsz, L, G, SSD_INNER // G)
    yg = yg * lax.rsqrt(jnp.mean(yg * yg, axis=-1, keepdims=True) + EPS)
    return (yg.reshape(Bsz, L, SSD_INNER) * norm_w.astype(f32)).astype(z.dtype)


def hgrn2_mixer(q, f_logit, i_in, g_out, lb, norm_w):
    f32 = jnp.float32
    Bsz, L, _ = q.shape
    H, K, V, C = HG_HEADS, HG_EXPAND, HG_HEAD_I, HG_CHUNK
    nc = L // C
    f = lb + (1.0 - lb) * jax.nn.sigmoid(f_logit.astype(f32))
    chunked = lambda t, d: jnp.moveaxis(t.reshape(Bsz, nc, C, H, d), 1, 0)
    qs = chunked(jax.nn.silu(q.astype(f32)), K)
    ks = chunked(1.0 - f, K)
    vs = chunked(i_in.astype(f32), V)
    gs = chunked(jnp.log(f), K)
    tri = jnp.tril(jnp.ones((C, C), dtype=bool))[None, :, :, None, None]

    def step(S, inp):
        qc, kc, vc, gc = inp
        Gc = jnp.cumsum(gc, axis=1)
        o_inter = jnp.einsum("blhk,bhkv->blhv", qc * jnp.exp(Gc), S)
        dec = jnp.exp(jnp.where(tri, Gc[:, :, None] - Gc[:, None, :], -jnp.inf))
        att = jnp.einsum("blhk,bshk,blshk->bhls", qc, kc, dec)
        o = o_inter + jnp.einsum("bhls,bshv->blhv", att, vc)
        G_last = Gc[:, -1]
        S_new = jnp.exp(G_last)[..., None] * S + jnp.einsum(
            "bshk,bshv->bhkv", kc * jnp.exp(G_last[:, None] - Gc), vc)
        return S_new, o

    _, o = lax.scan(step, jnp.zeros((Bsz, H, K, V), f32), (qs, ks, vs, gs))
    o = jnp.moveaxis(o, 0, 1).reshape(Bsz, L, H, V)
    o = o * lax.rsqrt(jnp.mean(o * o, axis=-1, keepdims=True) + EPS) * norm_w.astype(f32).reshape(H, V)
    o = o.reshape(Bsz, L, HG_WIDTH) * jax.nn.silu(g_out.astype(f32))
    return o.astype(q.dtype)


def setup_inputs(seed: int = 0) -> dict:
    key = jax.random.key(seed)
    ks = jax.random.split(key, 24)
    nrm = lambda k, shape, s: jax.random.normal(k, shape, jnp.float32) * s
    gain = lambda k, shape: 1.0 + 0.02 * jax.random.normal(k, shape, jnp.float32)
    dt0 = jnp.exp(jax.random.uniform(ks[9], (DEPTH, SSD_HEADS), jnp.float32, np.log(1e-3), np.log(1e-1)))
    return {
        "x": nrm(ks[0], (BATCH, SEQ, D_MODEL), 1.0),
        "meta_tokens": nrm(ks[1], (N_META, D_MODEL), 1.0),
        "ffn1_norm": gain(ks[2], (DEPTH, D_MODEL)),
        "ffn1_w_gu": nrm(ks[3], (DEPTH, D_MODEL, 2 * D_FF), D_MODEL ** -0.5),
        "ffn1_w_down": nrm(ks[4], (DEPTH, D_FF, D_MODEL), D_FF ** -0.5),
        "mix_norm": gain(ks[5], (DEPTH, D_MODEL)),
        "w_in": nrm(ks[6], (DEPTH, D_MODEL, IN_TOTAL), D_MODEL ** -0.5),
        "ssd_conv_w": nrm(ks[7], (DEPTH, SSD_CONV, SSD_CONV_CH), SSD_CONV ** -0.5),
        "ssd_conv_b": nrm(ks[8], (DEPTH, SSD_CONV_CH), 0.01),
        "ssd_dt_bias": dt0 + jnp.log(-jnp.expm1(-dt0)),
        "ssd_a_log": jnp.log(jax.random.uniform(ks[10], (DEPTH, SSD_HEADS), jnp.float32, 1.0, 16.0)),
        "ssd_d": 1.0 + 0.1 * jax.random.normal(ks[11], (DEPTH, SSD_HEADS), jnp.float32),
        "ssd_norm": gain(ks[12], (DEPTH, SSD_INNER)),
        "hg_lower_bound": 1.0 + 0.1 * jax.random.normal(ks[13], (DEPTH + 1, HG_WIDTH), jnp.float32),
        "hg_norm": gain(ks[14], (DEPTH, HG_WIDTH)),
        "w_branch_a": nrm(ks[15], (DEPTH, SSD_INNER, D_MODEL), SSD_INNER ** -0.5),
        "w_branch_b": nrm(ks[16], (DEPTH, HG_WIDTH, D_MODEL), HG_WIDTH ** -0.5),
        "w_out": nrm(ks[17], (DEPTH, D_MODEL, D_MODEL), D_MODEL ** -0.5),
        "ffn2_norm": gain(ks[18], (DEPTH, D_MODEL)),
        "ffn2_w_gu": nrm(ks[19], (DEPTH, D_MODEL, 2 * D_FF), D_MODEL ** -0.5),
        "ffn2_w_down": nrm(ks[20], (DEPTH, D_FF, D_MODEL), D_FF ** -0.5),
        "final_norm": gain(ks[21], (D_MODEL,)),
    }


def reference(x, meta_tokens, ffn1_norm, ffn1_w_gu, ffn1_w_down, mix_norm, w_in, ssd_conv_w, ssd_conv_b,
              ssd_dt_bias, ssd_a_log, ssd_d, ssd_norm, hg_lower_bound, hg_norm, w_branch_a, w_branch_b,
              w_out, ffn2_norm, ffn2_w_gu, ffn2_w_down, final_norm):
    Bsz = x.shape[0]
    meta = jnp.broadcast_to(meta_tokens[None].astype(x.dtype), (Bsz, N_META, D_MODEL))
    h = jnp.concatenate([meta, x], axis=1)
    lb_all = jnp.cumsum(jax.nn.softmax(hg_lower_bound.astype(jnp.float32), axis=0), axis=0)
    splits = _split_points()
    for l in range(DEPTH):
        h = h + 0.5 * swiglu(rmsnorm(h, ffn1_norm[l]), ffn1_w_gu[l], ffn1_w_down[l])
        u = rmsnorm(h, mix_norm[l])
        z, xbc, dt_raw, q, f_logit, i_in, g_out, gate_a, gate_b = jnp.split(u @ w_in[l], splits, axis=-1)
        y_a = ssd_mixer(z, xbc, dt_raw, ssd_conv_w[l], ssd_conv_b[l], ssd_dt_bias[l], ssd_a_log[l],
                        ssd_d[l], ssd_norm[l])
        y_b = hgrn2_mixer(q, f_logit, i_in, g_out, lb_all[l], hg_norm[l])
        merged = jax.nn.sigmoid(gate_a) * (y_a @ w_branch_a[l]) + jax.nn.sigmoid(gate_b) * (y_b @ w_branch_b[l])
        h = h + merged @ w_out[l]
        h = h + 0.5 * swiglu(rmsnorm(h, ffn2_norm[l]), ffn2_w_gu[l], ffn2_w_down[l])
    h = rmsnorm(h, final_norm)
    return h[:, N_META:]
```

```python
import functools

import numpy as np
import jax
import jax.numpy as jnp
from jax import lax
from jax.experimental import pallas as pl
from jax.experimental.pallas import tpu as pltpu

F32 = jnp.float32
BF16 = jnp.bfloat16
EPS = 1e-6

D_MODEL = 1024
D_FF = 2816
FF_CHUNK = 256
N_FF_CHUNKS = D_FF // FF_CHUNK

SSD_HEADS = 16
SSD_P = 64
SSD_G = 4
SSD_R = SSD_HEADS // SSD_G
SSD_N = 128
SSD_INNER = SSD_HEADS * SSD_P
SSD_CONV = 4
SSD_XBC = SSD_INNER + 2 * SSD_G * SSD_N
SSD_GW = SSD_R * SSD_P
LANES = 128
TAIL_ROWS = 8

HG_HEADS = 8
HG_K = 128
HG_WIDTH = HG_HEADS * HG_K

VMEM_LIMIT_BYTES = 56 * 1024 * 1024


def _params(sem):
    return pltpu.CompilerParams(dimension_semantics=sem, vmem_limit_bytes=VMEM_LIMIT_BYTES)


def _const_spec(shape):
    zeros = (0,) * len(shape)
    return pl.BlockSpec(shape, lambda *_: zeros, pipeline_mode=pl.Buffered(1))


def _rms(x, w):
    return x * lax.rsqrt(jnp.mean(x * x, axis=-1, keepdims=True) + EPS) * w


def _silu(x):
    return x * jax.nn.sigmoid(x)


def _split3(x):
    hi = x.astype(BF16)
    r1 = x - hi.astype(F32)
    mid = r1.astype(BF16)
    lo = (r1 - mid.astype(F32)).astype(BF16)
    return hi, mid, lo


def _dot(a, b):
    return jnp.dot(a, b, preferred_element_type=F32)


def _dot_nt(a, b):
    return lax.dot_general(a, b, (((1,), (1,)), ((), ())), preferred_element_type=F32)


def _dot_tn(a, b):
    return lax.dot_general(a, b, (((0,), (0,)), ((), ())), preferred_element_type=F32)


def _sel_dot(sel, x):
    hi, mid, lo = _split3(x)
    return _dot(sel, hi) + _dot(sel, mid) + _dot(sel, lo)


def _dot_sel(x, sel):
    hi, mid, lo = _split3(x)
    return _dot(hi, sel) + _dot(mid, sel) + _dot(lo, sel)


def _ffn_accumulate(xn, wg_ref, wu_ref, wd_ref, acc_ref):
    acc_ref[...] = jnp.zeros_like(acc_ref)

    def body(c, carry):
        g = _dot(xn, wg_ref[c])
        u = _dot(xn, wu_ref[c])
        hsw = (_silu(g) * u).astype(BF16)
        acc_ref[...] += _dot(hsw, wd_ref[c])
        return carry

    lax.fori_loop(0, N_FF_CHUNKS, body, 0)


def _ffn1_body(x_ref, nw_ref, wg_ref, wu_ref, wd_ref, nw2_ref, h_ref, u_ref, acc_ref):
    x = x_ref[...]
    xn = _rms(x, nw_ref[...]).astype(BF16)
    _ffn_accumulate(xn, wg_ref, wu_ref, wd_ref, acc_ref)
    h = x + 0.5 * acc_ref[...]
    h_ref[...] = h
    u_ref[...] = _rms(h, nw2_ref[...]).astype(BF16)


def _ffn1_call(x2, nw, wg, wu, wd, nw2, tm):
    m = x2.shape[0]
    row = lambda i: (i, 0)
    return pl.pallas_call(
        _ffn1_body,
        grid=(m // tm,),
        in_specs=[
            pl.BlockSpec((tm, D_MODEL), row),
            _const_spec((1, D_MODEL)),
            _const_spec(wg.shape), _const_spec(wu.shape), _const_spec(wd.shape),
            _const_spec((1, D_MODEL)),
        ],
        out_specs=[pl.BlockSpec((tm, D_MODEL), row), pl.BlockSpec((tm, D_MODEL), row)],
        out_shape=[jax.ShapeDtypeStruct((m, D_MODEL), F32), jax.ShapeDtypeStruct((m, D_MODEL), BF16)],
        scratch_shapes=[pltpu.VMEM((tm, D_MODEL), F32)],
        compiler_params=_params(("arbitrary",)),
        name="ffn1",
    )(x2, nw, wg, wu, wd, nw2)


def _inproj_body(u_ref, *refs):
    n = len(refs) // 2
    u = u_ref[...]
    for w_ref, o_ref in zip(refs[:n], refs[n:]):
        o_ref[...] = _dot(u, w_ref[...]).astype(o_ref.dtype)


def _inproj_call(u2, weights, out_dtypes, tm):
    m = u2.shape[0]
    row = lambda i: (i, 0)
    return pl.pallas_call(
        _inproj_body,
        grid=(m // tm,),
        in_specs=[pl.BlockSpec((tm, D_MODEL), row)] + [_const_spec(w.shape) for w in weights],
        out_specs=[pl.BlockSpec((tm, w.shape[1]), row) for w in weights],
        out_shape=[jax.ShapeDtypeStruct((m, w.shape[1]), dt) for w, dt in zip(weights, out_dtypes)],
        compiler_params=_params(("arbitrary",)),
        name="inproj",
    )(u2, *weights)


def _ssd_body(z_ref, xbc_ref, dt_ref, cw_ref, cb_ref, dtb_ref, alog_ref, dsk_ref, nw_ref,
              lincl_ref, eye_ref, expand_ref, st0_ref, tail0_ref,
              y_ref, stout_ref, tailout_ref, st_scr, xpad_scr, y_scr, *, q):
    c = pl.program_id(1)

    @pl.when(c == 0)
    def _():
        st_scr[...] = st0_ref[...]
        xpad_scr[0:TAIL_ROWS, :] = tail0_ref[...]

    xpad_scr[TAIL_ROWS:TAIL_ROWS + q, :] = xbc_ref[0].astype(F32)
    conv = cb_ref[...]
    for k in range(SSD_CONV):
        lo = TAIL_ROWS - (SSD_CONV - 1) + k
        conv = conv + cw_ref[k:k + 1, :] * xpad_scr[lo:lo + q, :]
    xpad_scr[TAIL_ROWS - (SSD_CONV - 1):TAIL_ROWS, :] = xpad_scr[TAIL_ROWS + q - (SSD_CONV - 1):TAIL_ROWS + q, :]
    xc = _silu(conv)
    xs = xc[:, :SSD_INNER]
    bm = xc[:, SSD_INNER:SSD_INNER + SSD_G * SSD_N]
    cm = xc[:, SSD_INNER + SSD_G * SSD_N:]

    dt = jax.nn.softplus(dt_ref[0] + dtb_ref[...])
    a = dt * (-jnp.exp(alog_ref[...]))
    a_cs = _sel_dot(lincl_ref[...], a)
    a_last = a_cs[q - 1:q, :]
    a_cs_t = sum(_dot_nt(eye_ref[...], t) for t in _split3(a_cs))
    ea = jnp.exp(a_cs)
    dt_x = _dot_sel(dt, expand_ref[...])
    ea_x = _dot_sel(ea, expand_ref[...])
    dtdec_x = _dot_sel(dt * jnp.exp(a_last - a_cs), expand_ref[...])

    li = lax.broadcasted_iota(jnp.int32, (q, q), 0)
    si = lax.broadcasted_iota(jnp.int32, (q, q), 1)
    causal = li >= si

    x_dt = (xs * dt_x).astype(BF16)
    x_dec = (xs * dtdec_x).astype(BF16)
    for g in range(SSD_G):
        gl = slice(g * SSD_GW, (g + 1) * SSD_GW)
        b_g = bm[:, g * SSD_N:(g + 1) * SSD_N].astype(BF16)
        c_g = cm[:, g * SSD_N:(g + 1) * SSD_N].astype(BF16)
        cb = _dot_nt(c_g, b_g)
        st = st_scr[g]
        y_off = _dot(c_g, st.astype(BF16)) * ea_x[:, gl]
        for r in range(SSD_R):
            h = g * SSD_R + r
            hl = slice(h * SSD_P, (h + 1) * SSD_P)
            seg = a_cs[:, h:h + 1] - a_cs_t[h:h + 1, :]
            m_h = (cb * jnp.exp(jnp.where(causal, seg, -jnp.inf))).astype(BF16)
            y_scr[:, hl] = _dot(m_h, x_dt[:, hl])
        y_scr[:, gl] += y_off
        st_scr[g] = st * ea_x[q - 1:q, gl] + _dot_tn(b_g, x_dec[:, gl])

    y = y_scr[...] + xs * dsk_ref[...]
    yg = y * _silu(z_ref[0].astype(F32))
    for g in range(SSD_G):
        gl = slice(g * SSD_GW, (g + 1) * SSD_GW)
        blk = yg[:, gl]
        blk = blk * lax.rsqrt(jnp.mean(blk * blk, axis=-1, keepdims=True) + EPS)
        y_ref[0, :, gl] = (blk * nw_ref[:, gl]).astype(y_ref.dtype)

    @pl.when(c == pl.num_programs(1) - 1)
    def _():
        stout_ref[0] = st_scr[...]
        tailout_ref[0] = xpad_scr[0:TAIL_ROWS, :]


def _ssd_call(z, xbc, dt, cw, cb, dtb, alog, dsk, nw, st0, tail0, q):
    bsz, seq, _ = z.shape
    assert seq % q == 0 and q >= SSD_CONV - 1
    lincl = jnp.asarray(np.tril(np.ones((q, q), np.float32)), BF16)
    eye = jnp.asarray(np.eye(LANES, dtype=np.float32), BF16)
    expand = np.zeros((LANES, SSD_INNER), np.float32)
    for h in range(SSD_HEADS):
        expand[h, h * SSD_P:(h + 1) * SSD_P] = 1.0
    expand = jnp.asarray(expand, BF16)
    blk = lambda b, c: (b, c, 0)
    per_b3 = lambda b, c: (b, 0, 0)
    per_b4 = lambda b, c: (b, 0, 0, 0)
    return pl.pallas_call(
        functools.partial(_ssd_body, q=q),
        grid=(bsz, seq // q),
        in_specs=[
            pl.BlockSpec((1, q, SSD_INNER), blk),
            pl.BlockSpec((1, q, SSD_XBC), blk),
            pl.BlockSpec((1, q, LANES), blk),
            _const_spec(cw.shape), _const_spec(cb.shape), _const_spec(dtb.shape), _const_spec(alog.shape),
            _const_spec(dsk.shape), _const_spec(nw.shape),
            _const_spec(lincl.shape), _const_spec(eye.shape), _const_spec(expand.shape),
            _const_spec(st0.shape), _const_spec(tail0.shape),
        ],
        out_specs=[
            pl.BlockSpec((1, q, SSD_INNER), blk),
            pl.BlockSpec((1, SSD_G, SSD_N, SSD_GW), per_b4),
            pl.BlockSpec((1, TAIL_ROWS, SSD_XBC), per_b3),
        ],
        out_shape=[
            jax.ShapeDtypeStruct((bsz, seq, SSD_INNER), BF16),
            jax.ShapeDtypeStruct((bsz, SSD_G, SSD_N, SSD_GW), F32),
            jax.ShapeDtypeStruct((bsz, TAIL_ROWS, SSD_XBC), F32),
        ],
        scratch_shapes=[
            pltpu.VMEM((SSD_G, SSD_N, SSD_GW), F32),
            pltpu.VMEM((TAIL_ROWS + q, SSD_XBC), F32),
            pltpu.VMEM((q, SSD_INNER), F32),
        ],
        compiler_params=_params(("arbitrary", "arbitrary")),
        name="ssd",
    )(z, xbc, dt, cw, cb, dtb, alog, dsk, nw, lincl, eye, expand, st0, tail0)


def _hg_levels(q):
    out, m = [], q // 2
    while m >= 1:
        out.append(m)
        m //= 2
    return out


def _hg_sum_matrices(q):
    t = np.arange(q)
    l, tt = t[:, None], t[None, :]
    mats = [tt <= l, tt > l]
    for m in _hg_levels(q):
        ref = (l // (2 * m)) * (2 * m) + m - 1
        lower = (l % (2 * m)) >= m
        mats.append(np.where(lower, (tt > ref) & (tt <= l), (tt > l) & (tt <= ref)))
    return np.concatenate(mats, 0).astype(np.float32)


def _hgrn2_body(q_ref, f_ref, i_ref, g_ref, lbp_ref, nw_ref, sums_ref, st0_ref,
                y_ref, stout_ref, st_scr, *, q, layer):
    c = pl.program_id(1)

    @pl.when(c == 0)
    def _():
        st_scr[...] = st0_ref[...]

    lbp = lbp_ref[...]
    e = jnp.exp(lbp - jnp.max(lbp, axis=0, keepdims=True))
    lb = jnp.sum(e[:layer + 1], axis=0, keepdims=True) / jnp.sum(e, axis=0, keepdims=True)

    f = lb + (1.0 - lb) * jax.nn.sigmoid(f_ref[0].astype(F32))
    kk = 1.0 - f
    qs = _silu(q_ref[0].astype(F32))
    v = i_ref[0]
    p = jnp.exp(_sel_dot(sums_ref[...], jnp.log(f)))
    p_inc, p_rev = p[0:q], p[q:2 * q]

    li = lax.broadcasted_iota(jnp.int32, (q, q), 0)
    si = lax.broadcasted_iota(jnp.int32, (q, q), 1)
    lxs = li ^ si
    levels = _hg_levels(q)

    qs_b = qs.astype(BF16)
    kk_b = kk.astype(BF16)
    qg = (qs * p_inc).astype(BF16)
    kd = (kk * p_rev).astype(BF16)
    ql = [(qs * p[(2 + j) * q:(3 + j) * q]).astype(BF16) for j in range(len(levels))]
    kl = [(kk * p[(2 + j) * q:(3 + j) * q]).astype(BF16) for j in range(len(levels))]

    for h in range(HG_HEADS):
        hl = slice(h * HG_K, (h + 1) * HG_K)
        att = jnp.where(li == si, _dot_nt(qs_b[:, hl], kk_b[:, hl]), 0.0)
        for j, m in enumerate(levels):
            valid = (li > si) & ((lxs >> int(np.log2(m))) == 1)
            att = jnp.where(valid, _dot_nt(ql[j][:, hl], kl[j][:, hl]), att)
        st = st_scr[h]
        o = _dot(att.astype(BF16), v[:, hl]) + _dot_nt(qg[:, hl], st.astype(BF16))
        st_scr[h] = st * p_inc[q - 1:q, hl] + _dot_tn(v[:, hl], kd[:, hl])
        o = o * lax.rsqrt(jnp.mean(o * o, axis=-1, keepdims=True) + EPS) * nw_ref[:, hl]
        y_ref[0, :, hl] = (o * _silu(g_ref[0, :, hl].astype(F32))).astype(y_ref.dtype)

    @pl.when(c == pl.num_programs(1) - 1)
    def _():
        stout_ref[0] = st_scr[...]


def _hgrn2_call(qq, ff, ii, gg, lbp, nw, st0, q, layer):
    bsz, seq, _ = qq.shape
    assert seq % q == 0 and q & (q - 1) == 0
    sums = jnp.asarray(_hg_sum_matrices(q), BF16)
    blk = lambda b, c: (b, c, 0)
    per_b4 = lambda b, c: (b, 0, 0, 0)
    tok = pl.BlockSpec((1, q, HG_WIDTH), blk)
    return pl.pallas_call(
        functools.partial(_hgrn2_body, q=q, layer=layer),
        grid=(bsz, seq // q),
        in_specs=[tok, tok, tok, tok, _const_spec(lbp.shape), _const_spec(nw.shape),
                  _const_spec(sums.shape), _const_spec(st0.shape)],
        out_specs=[tok, pl.BlockSpec((1, HG_HEADS, HG_K, HG_K), per_b4)],
        out_shape=[jax.ShapeDtypeStruct((bsz, seq, HG_WIDTH), BF16),
                   jax.ShapeDtypeStruct((bsz, HG_HEADS, HG_K, HG_K), F32)],
        scratch_shapes=[pltpu.VMEM((HG_HEADS, HG_K, HG_K), F32)],
        compiler_params=_params(("arbitrary", "arbitrary")),
        name="hgrn2",
    )(qq, ff, ii, gg, lbp, nw, sums, st0)


def _tail_body(h_ref, ya_ref, yb_ref, ga_ref, gb_ref, wa_ref, wb_ref, wo_ref,
               nw_ref, wg_ref, wu_ref, wd_ref, fnw_ref, o_ref, acc_ref):
    merged = (jax.nn.sigmoid(ga_ref[...].astype(F32)) * _dot(ya_ref[...], wa_ref[...])
              + jax.nn.sigmoid(gb_ref[...].astype(F32)) * _dot(yb_ref[...], wb_ref[...]))
    h = h_ref[...] + _dot(merged.astype(BF16), wo_ref[...])
    xn = _rms(h, nw_ref[...]).astype(BF16)
    _ffn_accumulate(xn, wg_ref, wu_ref, wd_ref, acc_ref)
    h = h + 0.5 * acc_ref[...]
    o_ref[...] = _rms(h, fnw_ref[...])


def _tail_call(h, ya, yb, ga, gb, wa, wb, wo, nw, wg, wu, wd, fnw, tm):
    m = h.shape[0]
    row = lambda i: (i, 0)
    tile = pl.BlockSpec((tm, D_MODEL), row)
    return pl.pallas_call(
        _tail_body,
        grid=(m // tm,),
        in_specs=[tile, tile, tile, tile, tile,
                  _const_spec(wa.shape), _const_spec(wb.shape), _const_spec(wo.shape),
                  _const_spec((1, D_MODEL)),
                  _const_spec(wg.shape), _const_spec(wu.shape), _const_spec(wd.shape),
                  _const_spec((1, D_MODEL))],
        out_specs=tile,
        out_shape=jax.ShapeDtypeStruct((m, D_MODEL), F32),
        scratch_shapes=[pltpu.VMEM((tm, D_MODEL), F32)],
        compiler_params=_params(("arbitrary",)),
        name="tail",
    )(h, ya, yb, ga, gb, wa, wb, wo, nw, wg, wu, wd, fnw)


def _ffn_weights(w_gu, w_down):
    wg = w_gu[:, :D_FF].reshape(D_MODEL, N_FF_CHUNKS, FF_CHUNK).transpose(1, 0, 2).astype(BF16)
    wu = w_gu[:, D_FF:].reshape(D_MODEL, N_FF_CHUNKS, FF_CHUNK).transpose(1, 0, 2).astype(BF16)
    wd = w_down.reshape(N_FF_CHUNKS, FF_CHUNK, D_MODEL).astype(BF16)
    return wg, wu, wd


def _pad_lanes(v):
    return jnp.zeros((1, LANES), F32).at[0, :v.shape[0]].set(v.astype(F32))


def _mixers(u, bsz, seq, q_ssd, q_hg, w_parts, ssd_p, hg_p, ssd_state, hg_state, tm):
    outs = _inproj_call(u, w_parts, [BF16, BF16, F32, BF16, BF16, BF16, BF16, BF16, BF16], tm)
    z, xbc, dt, hq, hf, hi, hg, ga, gb = outs
    r3 = lambda t: t.reshape(bsz, seq, t.shape[-1])
    ya, st_ssd, tail = _ssd_call(r3(z), r3(xbc), r3(dt), *ssd_p, *ssd_state, q=q_ssd)
    yb, st_hg = _hgrn2_call(r3(hq), r3(hf), r3(hi), r3(hg), *hg_p, hg_state, q=q_hg, layer=0)
    return ya, yb, ga, gb, (st_ssd, tail), st_hg


def kernel(x, meta_tokens, ffn1_norm, ffn1_w_gu, ffn1_w_down, mix_norm, w_in, ssd_conv_w, ssd_conv_b,
           ssd_dt_bias, ssd_a_log, ssd_d, ssd_norm, hg_lower_bound, hg_norm, w_branch_a, w_branch_b,
           w_out, ffn2_norm, ffn2_w_gu, ffn2_w_down, final_norm):
    bsz, seq, d = x.shape
    assert d == D_MODEL and ffn1_norm.shape[0] == 1, "single-layer block of width D_MODEL only"
    n_meta = meta_tokens.shape[0]
    row = lambda v: v.reshape(1, -1).astype(F32)

    wg1, wu1, wd1 = _ffn_weights(ffn1_w_gu[0], ffn1_w_down[0])
    wg2, wu2, wd2 = _ffn_weights(ffn2_w_gu[0], ffn2_w_down[0])
    sizes = (SSD_INNER, SSD_XBC, SSD_HEADS, HG_WIDTH, HG_WIDTH, HG_WIDTH, HG_WIDTH, D_MODEL, D_MODEL)
    offs = np.cumsum((0,) + sizes)
    w_parts = [w_in[0][:, offs[j]:offs[j + 1]].astype(BF16) for j in range(len(sizes))]
    w_parts[2] = jnp.pad(w_parts[2], ((0, 0), (0, LANES - SSD_HEADS)))

    ssd_p = (ssd_conv_w[0].astype(F32), row(ssd_conv_b[0]), _pad_lanes(ssd_dt_bias[0]), _pad_lanes(ssd_a_log[0]),
             row(jnp.repeat(ssd_d[0], SSD_P)), row(ssd_norm[0]))
    hg_p = (hg_lower_bound.astype(F32), row(hg_norm[0]))

    h_m, u_m = _ffn1_call(meta_tokens.astype(F32), row(ffn1_norm[0]), wg1, wu1, wd1, row(mix_norm[0]), n_meta)
    zero_ssd = (jnp.zeros((SSD_G, SSD_N, SSD_GW), F32), jnp.zeros((TAIL_ROWS, SSD_XBC), F32))
    zero_hg = jnp.zeros((HG_HEADS, HG_K, HG_K), F32)
    _, _, _, _, (st_ssd, tail), st_hg = _mixers(u_m, 1, n_meta, n_meta, n_meta, w_parts, ssd_p, hg_p,
                                                zero_ssd, zero_hg, n_meta)

    tm = 512
    x2 = x.reshape(bsz * seq, d)
    h1, u = _ffn1_call(x2, row(ffn1_norm[0]), wg1, wu1, wd1, row(mix_norm[0]), tm)
    ya, yb, ga, gb, _, _ = _mixers(u, bsz, seq, 128, 128, w_parts, ssd_p, hg_p,
                                   (st_ssd[0], tail[0]), st_hg[0], tm)
    out = _tail_call(h1, ya.reshape(bsz * seq, -1), yb.reshape(bsz * seq, -1), ga, gb,
                     w_branch_a[0].astype(BF16), w_branch_b[0].astype(BF16), w_out[0].astype(BF16),
                     row(ffn2_norm[0]), wg2, wu2, wd2, row(final_norm), tm)
    return out.reshape(bsz, seq, d)
```

```python
import functools

import numpy as np
import jax
import jax.numpy as jnp
from jax import lax
from jax.experimental import pallas as pl
from jax.experimental.pallas import tpu as pltpu

F32 = jnp.float32
BF16 = jnp.bfloat16
EPS = 1e-6

D_MODEL = 1024
D_FF = 2816
FF_CHUNK = 256
N_FF_CHUNKS = D_FF // FF_CHUNK

SSD_HEADS = 16
SSD_P = 64
SSD_G = 4
SSD_R = SSD_HEADS // SSD_G
SSD_N = 128
SSD_INNER = SSD_HEADS * SSD_P
SSD_CONV = 4
SSD_XBC = SSD_INNER + 2 * SSD_G * SSD_N
SSD_GW = SSD_R * SSD_P
LANES = 128
TAIL_ROWS = 8

HG_HEADS = 8
HG_K = 128
HG_WIDTH = HG_HEADS * HG_K

VMEM_LIMIT_BYTES = 56 * 1024 * 1024


def _params(sem):
    return pltpu.CompilerParams(dimension_semantics=sem, vmem_limit_bytes=VMEM_LIMIT_BYTES)


def _const_spec(shape):
    zeros = (0,) * len(shape)
    return pl.BlockSpec(shape, lambda *_: zeros, pipeline_mode=pl.Buffered(1))


def _rms(x, w):
    return x * lax.rsqrt(jnp.mean(x * x, axis=-1, keepdims=True) + EPS) * w


def _silu(x):
    return x * jax.nn.sigmoid(x)


def _split3(x):
    hi = x.astype(BF16)
    r1 = x - hi.astype(F32)
    mid = r1.astype(BF16)
    lo = (r1 - mid.astype(F32)).astype(BF16)
    return hi, mid, lo


def _dot(a, b):
    return jnp.dot(a, b, preferred_element_type=F32)


def _dot_nt(a, b):
    return lax.dot_general(a, b, (((1,), (1,)), ((), ())), preferred_element_type=F32)


def _dot_tn(a, b):
    return lax.dot_general(a, b, (((0,), (0,)), ((), ())), preferred_element_type=F32)


def _sel_dot(sel, x):
    hi, mid, lo = _split3(x)
    return _dot(sel, hi) + _dot(sel, mid) + _dot(sel, lo)


def _dot_sel(x, sel):
    hi, mid, lo = _split3(x)
    return _dot(hi, sel) + _dot(mid, sel) + _dot(lo, sel)


def _ffn_accumulate(xn, wg_ref, wu_ref, wd_ref, acc_ref):
    acc_ref[...] = jnp.zeros_like(acc_ref)

    def body(c, carry):
        g = _dot(xn, wg_ref[c])
        u = _dot(xn, wu_ref[c])
        hsw = (_silu(g) * u).astype(BF16)
        acc_ref[...] += _dot(hsw, wd_ref[c])
        return carry

    lax.fori_loop(0, N_FF_CHUNKS, body, 0, unroll=True)


def _ffn1_body(x_ref, nw_ref, wg_ref, wu_ref, wd_ref, nw2_ref, h_ref, u_ref, acc_ref):
    x = x_ref[...]
    xn = _rms(x, nw_ref[...]).astype(BF16)
    _ffn_accumulate(xn, wg_ref, wu_ref, wd_ref, acc_ref)
    h = x + acc_ref[...]
    h_ref[...] = h
    u_ref[...] = _rms(h, nw2_ref[...]).astype(BF16)


def _ffn1_call(x2, nw, wg, wu, wd, nw2, tm):
    m = x2.shape[0]
    row = lambda i: (i, 0)
    return pl.pallas_call(
        _ffn1_body,
        grid=(m // tm,),
        in_specs=[
            pl.BlockSpec((tm, D_MODEL), row),
            _const_spec((1, D_MODEL)),
            _const_spec(wg.shape), _const_spec(wu.shape), _const_spec(wd.shape),
            _const_spec((1, D_MODEL)),
        ],
        out_specs=[pl.BlockSpec((tm, D_MODEL), row), pl.BlockSpec((tm, D_MODEL), row)],
        out_shape=[jax.ShapeDtypeStruct((m, D_MODEL), F32), jax.ShapeDtypeStruct((m, D_MODEL), BF16)],
        scratch_shapes=[pltpu.VMEM((tm, D_MODEL), F32)],
        compiler_params=_params(("arbitrary",)),
        name="ffn1",
    )(x2, nw, wg, wu, wd, nw2)


def _inproj_body(u_ref, *refs):
    n = len(refs) // 2
    u = u_ref[...]
    for w_ref, o_ref in zip(refs[:n], refs[n:]):
        o_ref[...] = _dot(u, w_ref[...]).astype(o_ref.dtype)


def _inproj_call(u2, weights, out_dtypes, tm):
    m = u2.shape[0]
    row = lambda i: (i, 0)
    return pl.pallas_call(
        _inproj_body,
        grid=(m // tm,),
        in_specs=[pl.BlockSpec((tm, D_MODEL), row)] + [_const_spec(w.shape) for w in weights],
        out_specs=[pl.BlockSpec((tm, w.shape[1]), row) for w in weights],
        out_shape=[jax.ShapeDtypeStruct((m, w.shape[1]), dt) for w, dt in zip(weights, out_dtypes)],
        compiler_params=_params(("arbitrary",)),
        name="inproj",
    )(u2, *weights)


def _ssd_body(z_ref, xbc_ref, dt_ref, cw_ref, cb_ref, dtb_ref, alog_ref, dsk_ref, nw_ref,
              lincl_ref, eye_ref, expand_ref, st0_ref, tail0_ref,
              y_ref, stout_ref, tailout_ref, st_scr, xpad_scr, y_scr, *, q):
    c = pl.program_id(1)

    @pl.when(c == 0)
    def _():
        st_scr[...] = st0_ref[...]
        xpad_scr[0:TAIL_ROWS, :] = tail0_ref[...]

    xpad_scr[TAIL_ROWS:TAIL_ROWS + q, :] = xbc_ref[0].astype(F32)
    conv = cb_ref[...]
    for k in range(SSD_CONV):
        lo = TAIL_ROWS - (SSD_CONV - 1) + k
        conv = conv + cw_ref[k:k + 1, :] * xpad_scr[lo:lo + q, :]
    xpad_scr[TAIL_ROWS - (SSD_CONV - 1):TAIL_ROWS, :] = xpad_scr[TAIL_ROWS + q - (SSD_CONV - 1):TAIL_ROWS + q, :]
    xc = _silu(conv)
    xs = xc[:, :SSD_INNER]
    bm = xc[:, SSD_INNER:SSD_INNER + SSD_G * SSD_N]
    cm = xc[:, SSD_INNER + SSD_G * SSD_N:]

    dt = jax.nn.softplus(dt_ref[0] + dtb_ref[...])
    a = dt * (-jnp.exp(alog_ref[...]))
    a_cs = _sel_dot(lincl_ref[...], a)
    a_last = a_cs[q - 1:q, :]
    a_cs_t = sum(_dot_nt(eye_ref[...], t) for t in _split3(a_cs))
    ea = jnp.exp(a_cs)
    dt_x = _dot_sel(dt, expand_ref[...])
    ea_x = _dot_sel(ea, expand_ref[...])
    dtdec_x = _dot_sel(dt * jnp.exp(a_last - a_cs), expand_ref[...])

    li = lax.broadcasted_iota(jnp.int32, (q, q), 0)
    si = lax.broadcasted_iota(jnp.int32, (q, q), 1)
    causal = li >= si

    x_dt = (xs * dt_x).astype(BF16)
    x_dec = (xs * dtdec_x).astype(BF16)
    for g in range(SSD_G):
        gl = slice(g * SSD_GW, (g + 1) * SSD_GW)
        b_g = bm[:, g * SSD_N:(g + 1) * SSD_N].astype(BF16)
        c_g = cm[:, g * SSD_N:(g + 1) * SSD_N].astype(BF16)
        cb = _dot_nt(c_g, b_g)
        st = st_scr[g]
        y_off = _dot(c_g, st.astype(BF16)) * ea_x[:, gl]
        for r in range(SSD_R):
            h = g * SSD_R + r
            hl = slice(h * SSD_P, (h + 1) * SSD_P)
            seg = a_cs[:, h:h + 1] - a_cs_t[h:h + 1, :]
            m_h = (cb * jnp.exp(jnp.where(causal, seg, -jnp.inf))).astype(BF16)
            y_scr[:, hl] = _dot(m_h, x_dt[:, hl])
        y_scr[:, gl] += y_off
        st_scr[g] = st * ea_x[q - 1:q, gl] + _dot_tn(b_g, x_dec[:, gl])

    y = y_scr[...] + xs * dsk_ref[...]
    yg = y * _silu(z_ref[0].astype(F32))
    for g in range(SSD_G):
        gl = slice(g * SSD_GW, (g + 1) * SSD_GW)
        blk = yg[:, gl]
        blk = blk * lax.rsqrt(jnp.mean(blk * blk, axis=-1, keepdims=True) + EPS)
        y_ref[0, :, gl] = (blk * nw_ref[:, gl]).astype(y_ref.dtype)

    @pl.when(c == pl.num_programs(1) - 1)
    def _():
        stout_ref[0] = st_scr[...]
        tailout_ref[0] = xpad_scr[0:TAIL_ROWS, :]


def _ssd_call(z, xbc, dt, cw, cb, dtb, alog, dsk, nw, st0, tail0, q):
    bsz, seq, _ = z.shape
    assert seq % q == 0 and q >= SSD_CONV - 1
    lincl = jnp.asarray(np.tril(np.ones((q, q), np.float32)), BF16)
    eye = jnp.asarray(np.eye(LANES, dtype=np.float32), BF16)
    expand = np.zeros((LANES, SSD_INNER), np.float32)
    for h in range(SSD_HEADS):
        expand[h, h * SSD_P:(h + 1) * SSD_P] = 1.0
    expand = jnp.asarray(expand, BF16)
    blk = lambda b, c: (b, c, 0)
    per_b3 = lambda b, c: (b, 0, 0)
    per_b4 = lambda b, c: (b, 0, 0, 0)
    return pl.pallas_call(
        functools.partial(_ssd_body, q=q),
        grid=(bsz, seq // q),
        in_specs=[
            pl.BlockSpec((1, q, SSD_INNER), blk),
            pl.BlockSpec((1, q, SSD_XBC), blk),
            pl.BlockSpec((1, q, LANES), blk),
            _const_spec(cw.shape), _const_spec(cb.shape), _const_spec(dtb.shape), _const_spec(alog.shape),
            _const_spec(dsk.shape), _const_spec(nw.shape),
            _const_spec(lincl.shape), _const_spec(eye.shape), _const_spec(expand.shape),
            _const_spec(st0.shape), _const_spec(tail0.shape),
        ],
        out_specs=[
            pl.BlockSpec((1, q, SSD_INNER), blk),
            pl.BlockSpec((1, SSD_G, SSD_N, SSD_GW), per_b4),
            pl.BlockSpec((1, TAIL_ROWS, SSD_XBC), per_b3),
        ],
        out_shape=[
            jax.ShapeDtypeStruct((bsz, seq, SSD_INNER), BF16),
            jax.ShapeDtypeStruct((bsz, SSD_G, SSD_N, SSD_GW), F32),
            jax.ShapeDtypeStruct((bsz, TAIL_ROWS, SSD_XBC), F32),
        ],
        scratch_shapes=[
            pltpu.VMEM((SSD_G, SSD_N, SSD_GW), F32),
            pltpu.VMEM((TAIL_ROWS + q, SSD_XBC), F32),
            pltpu.VMEM((q, SSD_INNER), F32),
        ],
        compiler_params=_params(("arbitrary", "arbitrary")),
        name="ssd",
    )(z, xbc, dt, cw, cb, dtb, alog, dsk, nw, lincl, eye, expand, st0, tail0)


def _hg_levels(q):
    out, m = [], q // 2
    while m >= 1:
        out.append(m)
        m //= 2
    return out


def _hg_sum_matrices(q):
    t = np.arange(q)
    l, tt = t[:, None], t[None, :]
    mats = [tt <= l, tt > l]
    for m in _hg_levels(q):
        ref = (l // (2 * m)) * (2 * m) + m - 1
        lower = (l % (2 * m)) >= m
        mats.append(np.where(lower, (tt > ref) & (tt <= l), (tt > l) & (tt <= ref)))
    return np.concatenate(mats, 0).astype(np.float32)


def _hg_level_masks(q):
    t = np.arange(q)
    l, s = t[:, None], t[None, :]
    masks = [(l > s) & (((l ^ s) >> int(np.log2(m))) == 1) for m in _hg_levels(q)] + [l == s]
    return np.stack([np.concatenate([mk, mk], 1) for mk in masks]).astype(np.float32)


def _block_diag_pair(x):
    z = jnp.zeros((x.shape[0], HG_K), x.dtype)
    return jnp.concatenate([jnp.concatenate([x[:, :HG_K], z], 1), jnp.concatenate([z, x[:, HG_K:]], 1)], 0)


def _hgrn2_body(q_ref, f_ref, i_ref, g_ref, lbp_ref, nw_ref, sums_ref, masks_ref, st0_ref,
                y_ref, stout_ref, st_scr, *, q, layer):
    c = pl.program_id(1)

    @pl.when(c == 0)
    def _():
        st_scr[...] = st0_ref[...]

    lbp = lbp_ref[...]
    e = jnp.exp(lbp - jnp.max(lbp, axis=0, keepdims=True))
    lb = jnp.sum(e[:layer + 1], axis=0, keepdims=True) / jnp.sum(e, axis=0, keepdims=True)

    f = lb + (1.0 - lb) * jax.nn.sigmoid(f_ref[0].astype(F32))
    kk_b = (1.0 - f).astype(BF16)
    qs_b = _silu(q_ref[0].astype(F32)).astype(BF16)
    v = i_ref[0]
    lf = jnp.log(f) * np.float32(np.log2(np.e))
    hi = lf.astype(BF16)
    mid = (lf - hi.astype(F32)).astype(BF16)
    p = jnp.exp2(_dot(sums_ref[...], jnp.concatenate([hi, mid], axis=0)))
    p_last = p[q - 1:q]
    p = p.astype(BF16)
    n_lev = len(_hg_levels(q))
    wq = [qs_b * p[(2 + j) * q:(3 + j) * q] for j in range(n_lev)] + [qs_b]
    wk = [kk_b * p[(2 + j) * q:(3 + j) * q] for j in range(n_lev)] + [kk_b]
    qg = qs_b * p[0:q]
    kd = kk_b * p[q:2 * q]

    for pair in range(HG_HEADS // 2):
        pl2 = slice(2 * pair * HG_K, 2 * (pair + 1) * HG_K)
        att = jnp.zeros((q, 2 * q), BF16)
        for j in range(n_lev + 1):
            a_j = _dot_nt(wq[j][:, pl2], _block_diag_pair(wk[j][:, pl2]))
            att = att + a_j.astype(BF16) * masks_ref[j]
        st = st_scr[pair]
        o = _dot(att, _block_diag_pair(v[:, pl2])) + _dot_nt(qg[:, pl2], st.astype(BF16))
        for hh in range(2):
            h = 2 * pair + hh
            hl = slice(h * HG_K, (h + 1) * HG_K)
            bl = slice(hh * HG_K, (hh + 1) * HG_K)
            st_scr[pair, bl, bl] = st[bl, bl] * p_last[:, hl] + _dot_tn(v[:, hl], kd[:, hl])
            o_h = o[:, bl]
            o_h = o_h * lax.rsqrt(jnp.mean(o_h * o_h, axis=-1, keepdims=True) + EPS) * nw_ref[:, hl]
            y_ref[0, :, hl] = (o_h * _silu(g_ref[0, :, hl].astype(F32))).astype(y_ref.dtype)

    @pl.when(c == pl.num_programs(1) - 1)
    def _():
        stout_ref[0] = st_scr[...]


def _hgrn2_call(qq, ff, ii, gg, lbp, nw, st0, q, layer):
    bsz, seq, _ = qq.shape
    assert seq % q == 0 and q & (q - 1) == 0
    sums = _hg_sum_matrices(q)
    sums = jnp.asarray(np.concatenate([sums, sums], 1), BF16)
    masks = jnp.asarray(_hg_level_masks(q), BF16)
    blk = lambda b, c: (b, c, 0)
    per_b4 = lambda b, c: (b, 0, 0, 0)
    tok = pl.BlockSpec((1, q, HG_WIDTH), blk)
    st_shape = (HG_HEADS // 2, 2 * HG_K, 2 * HG_K)
    return pl.pallas_call(
        functools.partial(_hgrn2_body, q=q, layer=layer),
        grid=(bsz, seq // q),
        in_specs=[tok, tok, tok, tok, _const_spec(lbp.shape), _const_spec(nw.shape),
                  _const_spec(sums.shape), _const_spec(masks.shape), _const_spec(st0.shape)],
        out_specs=[tok, pl.BlockSpec((1,) + st_shape, per_b4)],
        out_shape=[jax.ShapeDtypeStruct((bsz, seq, HG_WIDTH), BF16),
                   jax.ShapeDtypeStruct((bsz,) + st_shape, F32)],
        scratch_shapes=[pltpu.VMEM(st_shape, F32)],
        compiler_params=_params(("arbitrary", "arbitrary")),
        name="hgrn2",
    )(qq, ff, ii, gg, lbp, nw, sums, masks, st0)


def _tail_body(h_ref, ya_ref, yb_ref, ga_ref, gb_ref, wa_ref, wb_ref, wo_ref,
               nw_ref, wg_ref, wu_ref, wd_ref, fnw_ref, o_ref, acc_ref):
    merged = (jax.nn.sigmoid(ga_ref[...].astype(F32)) * _dot(ya_ref[...], wa_ref[...])
              + jax.nn.sigmoid(gb_ref[...].astype(F32)) * _dot(yb_ref[...], wb_ref[...]))
    h = h_ref[...] + _dot(merged.astype(BF16), wo_ref[...])
    xn = _rms(h, nw_ref[...]).astype(BF16)
    _ffn_accumulate(xn, wg_ref, wu_ref, wd_ref, acc_ref)
    h = h + acc_ref[...]
    o_ref[...] = _rms(h, fnw_ref[...])


def _tail_call(h, ya, yb, ga, gb, wa, wb, wo, nw, wg, wu, wd, fnw, tm):
    m = h.shape[0]
    row = lambda i: (i, 0)
    tile = pl.BlockSpec((tm, D_MODEL), row)
    return pl.pallas_call(
        _tail_body,
        grid=(m // tm,),
        in_specs=[tile, tile, tile, tile, tile,
                  _const_spec(wa.shape), _const_spec(wb.shape), _const_spec(wo.shape),
                  _const_spec((1, D_MODEL)),
                  _const_spec(wg.shape), _const_spec(wu.shape), _const_spec(wd.shape),
                  _const_spec((1, D_MODEL))],
        out_specs=tile,
        out_shape=jax.ShapeDtypeStruct((m, D_MODEL), F32),
        scratch_shapes=[pltpu.VMEM((tm, D_MODEL), F32)],
        compiler_params=_params(("arbitrary",)),
        name="tail",
    )(h, ya, yb, ga, gb, wa, wb, wo, nw, wg, wu, wd, fnw)


def _ffn_weights(w_gu, w_down):
    wg = w_gu[:, :D_FF].reshape(D_MODEL, N_FF_CHUNKS, FF_CHUNK).transpose(1, 0, 2).astype(BF16)
    wu = w_gu[:, D_FF:].reshape(D_MODEL, N_FF_CHUNKS, FF_CHUNK).transpose(1, 0, 2).astype(BF16)
    wd = (0.5 * w_down).reshape(N_FF_CHUNKS, FF_CHUNK, D_MODEL).astype(BF16)
    return wg, wu, wd


def _pad_lanes(v):
    return jnp.zeros((1, LANES), F32).at[0, :v.shape[0]].set(v.astype(F32))


def _mixers(u, bsz, seq, q_ssd, q_hg, w_parts, ssd_p, hg_p, ssd_state, hg_state, tm):
    outs = _inproj_call(u, w_parts, [BF16, BF16, F32, BF16, BF16, BF16, BF16, BF16, BF16], tm)
    z, xbc, dt, hq, hf, hi, hg, ga, gb = outs
    r3 = lambda t: t.reshape(bsz, seq, t.shape[-1])
    ya, st_ssd, tail = _ssd_call(r3(z), r3(xbc), r3(dt), *ssd_p, *ssd_state, q=q_ssd)
    yb, st_hg = _hgrn2_call(r3(hq), r3(hf), r3(hi), r3(hg), *hg_p, hg_state, q=q_hg, layer=0)
    return ya, yb, ga, gb, (st_ssd, tail), st_hg


def kernel(x, meta_tokens, ffn1_norm, ffn1_w_gu, ffn1_w_down, mix_norm, w_in, ssd_conv_w, ssd_conv_b,
           ssd_dt_bias, ssd_a_log, ssd_d, ssd_norm, hg_lower_bound, hg_norm, w_branch_a, w_branch_b,
           w_out, ffn2_norm, ffn2_w_gu, ffn2_w_down, final_norm):
    bsz, seq, d = x.shape
    assert d == D_MODEL and ffn1_norm.shape[0] == 1, "single-layer block of width D_MODEL only"
    n_meta = meta_tokens.shape[0]
    row = lambda v: v.reshape(1, -1).astype(F32)

    wg1, wu1, wd1 = _ffn_weights(ffn1_w_gu[0], ffn1_w_down[0])
    wg2, wu2, wd2 = _ffn_weights(ffn2_w_gu[0], ffn2_w_down[0])
    sizes = (SSD_INNER, SSD_XBC, SSD_HEADS, HG_WIDTH, HG_WIDTH, HG_WIDTH, HG_WIDTH, D_MODEL, D_MODEL)
    offs = np.cumsum((0,) + sizes)
    w_parts = [w_in[0][:, offs[j]:offs[j + 1]].astype(BF16) for j in range(len(sizes))]
    w_parts[2] = jnp.pad(w_parts[2], ((0, 0), (0, LANES - SSD_HEADS)))

    ssd_p = (ssd_conv_w[0].astype(F32), row(ssd_conv_b[0]), _pad_lanes(ssd_dt_bias[0]), _pad_lanes(ssd_a_log[0]),
             row(jnp.repeat(ssd_d[0], SSD_P)), row(ssd_norm[0]))
    hg_p = (hg_lower_bound.astype(F32), row(hg_norm[0]))

    h_m, u_m = _ffn1_call(meta_tokens.astype(F32), row(ffn1_norm[0]), wg1, wu1, wd1, row(mix_norm[0]), n_meta)
    zero_ssd = (jnp.zeros((SSD_G, SSD_N, SSD_GW), F32), jnp.zeros((TAIL_ROWS, SSD_XBC), F32))
    zero_hg = jnp.zeros((HG_HEADS // 2, 2 * HG_K, 2 * HG_K), F32)
    _, _, _, _, (st_ssd, tail), st_hg = _mixers(u_m, 1, n_meta, n_meta, n_meta, w_parts, ssd_p, hg_p,
                                                zero_ssd, zero_hg, n_meta)

    tm = 512
    x2 = x.reshape(bsz * seq, d)
    h1, u = _ffn1_call(x2, row(ffn1_norm[0]), wg1, wu1, wd1, row(mix_norm[0]), tm)
    ya, yb, ga, gb, _, _ = _mixers(u, bsz, seq, 128, 128, w_parts, ssd_p, hg_p,
                                   (st_ssd[0], tail[0]), st_hg[0], tm)
    out = _tail_call(h1, ya.reshape(bsz * seq, -1), yb.reshape(bsz * seq, -1), ga, gb,
                     w_branch_a[0].astype(BF16), w_branch_b[0].astype(BF16), w_out[0].astype(BF16),
                     row(ffn2_norm[0]), wg2, wu2, wd2, row(final_norm), tm)
    return out.reshape(bsz, seq, d)
```

```python
import functools
from typing import NamedTuple

import numpy as np
import jax
import jax.numpy as jnp
from jax import lax
from jax.experimental import pallas as pl
from jax.experimental.pallas import tpu as pltpu

F32 = jnp.float32
BF16 = jnp.bfloat16
EPS = 1e-6

D_MODEL = 1024
D_FF = 2816
FF_CHUNK = 256
N_FF_CHUNKS = D_FF // FF_CHUNK

SSD_HEADS = 16
SSD_P = 64
SSD_G = 4
SSD_R = SSD_HEADS // SSD_G
SSD_N = 128
SSD_INNER = SSD_HEADS * SSD_P
SSD_CONV = 4
SSD_XBC = SSD_INNER + 2 * SSD_G * SSD_N
SSD_GW = SSD_R * SSD_P
LANES = 128
TAIL_ROWS = 8
CUMSUM_TERMS = 3
EXPAND_TERMS = 2

HG_HEADS = 8
HG_K = 128
HG_WIDTH = HG_HEADS * HG_K

VMEM_LIMIT_BYTES = 56 * 1024 * 1024


def _params(sem):
    return pltpu.CompilerParams(dimension_semantics=sem, vmem_limit_bytes=VMEM_LIMIT_BYTES)


def _const_spec(shape):
    zeros = (0,) * len(shape)
    return pl.BlockSpec(shape, lambda *_: zeros, pipeline_mode=pl.Buffered(1))


def _rms(x, w):
    return x * lax.rsqrt(jnp.mean(x * x, axis=-1, keepdims=True) + EPS) * w


def _silu(x):
    return x * jax.nn.sigmoid(x)


def _split_terms(x, n):
    terms, rest = [], x
    for i in range(n):
        terms.append(rest.astype(BF16))
        if i + 1 < n:
            rest = rest - terms[-1].astype(F32)
    return terms


def _dot(a, b):
    return jnp.dot(a, b, preferred_element_type=F32)


def _dot_nt(a, b):
    return lax.dot_general(a, b, (((1,), (1,)), ((), ())), preferred_element_type=F32)


def _dot_tn(a, b):
    return lax.dot_general(a, b, (((0,), (0,)), ((), ())), preferred_element_type=F32)


def _stack(mat, n, axis):
    return np.concatenate([mat] * n, axis=axis)


def _ffn_accumulate(xn, wgu_ref, wd_ref, acc_ref):
    for c in range(N_FF_CHUNKS):
        g = _dot(xn, wgu_ref[:, c * FF_CHUNK:(c + 1) * FF_CHUNK])
        u = _dot(xn, wgu_ref[:, D_FF + c * FF_CHUNK:D_FF + (c + 1) * FF_CHUNK])
        d = _dot((_silu(g) * u).astype(BF16), wd_ref[c * FF_CHUNK:(c + 1) * FF_CHUNK, :])
        if c == 0:
            acc_ref[...] = d
        else:
            acc_ref[...] += d


def _ffn1_body(x_ref, nw_ref, wgu_ref, wd_ref, nw2_ref, h_ref, u_ref, acc_ref):
    x = x_ref[...]
    xn = _rms(x, nw_ref[...]).astype(BF16)
    _ffn_accumulate(xn, wgu_ref, wd_ref, acc_ref)
    h = x + acc_ref[...]
    h_ref[...] = h
    u_ref[...] = _rms(h, nw2_ref[...]).astype(BF16)


def _ffn1_call(x2, nw, wgu, wd, nw2, tm):
    m = x2.shape[0]
    row = lambda i: (i, 0)
    return pl.pallas_call(
        _ffn1_body,
        grid=(m // tm,),
        in_specs=[
            pl.BlockSpec((tm, D_MODEL), row),
            _const_spec((1, D_MODEL)),
            _const_spec(wgu.shape), _const_spec(wd.shape),
            _const_spec((1, D_MODEL)),
        ],
        out_specs=[pl.BlockSpec((tm, D_MODEL), row), pl.BlockSpec((tm, D_MODEL), row)],
        out_shape=[jax.ShapeDtypeStruct((m, D_MODEL), F32), jax.ShapeDtypeStruct((m, D_MODEL), BF16)],
        scratch_shapes=[pltpu.VMEM((tm, D_MODEL), F32)],
        compiler_params=_params(("arbitrary",)),
        name="ffn1",
    )(x2, nw, wgu, wd, nw2)


def _inproj_body(u_ref, *refs):
    n = len(refs) // 2
    u = u_ref[...]
    for w_ref, o_ref in zip(refs[:n], refs[n:]):
        o_ref[...] = _dot(u, w_ref[...]).astype(o_ref.dtype)


def _inproj_call(u2, weights, out_dtypes, tm):
    m = u2.shape[0]
    row = lambda i: (i, 0)
    return pl.pallas_call(
        _inproj_body,
        grid=(m // tm,),
        in_specs=[pl.BlockSpec((tm, D_MODEL), row)] + [_const_spec(w.shape) for w in weights],
        out_specs=[pl.BlockSpec((tm, w.shape[1]), row) for w in weights],
        out_shape=[jax.ShapeDtypeStruct((m, w.shape[1]), dt) for w, dt in zip(weights, out_dtypes)],
        compiler_params=_params(("arbitrary",)),
        name="inproj",
    )(u2, *weights)


def _ssd_body(z_ref, xbc_ref, dt_ref, cw_ref, cb_ref, dtb_ref, alog_ref, dsk_ref, nw_ref,
              lincl_ref, eye_ref, expand_ref, hmask_ref, st0_ref, tail0_ref,
              y_ref, stout_ref, tailout_ref, st_scr, xpad_scr, xbd_scr, *, q, n_sub):
    c = pl.program_id(1)

    @pl.when(c == 0)
    def _():
        st_scr[...] = st0_ref[...]
        xpad_scr[0:TAIL_ROWS, :] = tail0_ref[...]

    t = n_sub * q
    xpad_scr[TAIL_ROWS:TAIL_ROWS + t, :] = xbc_ref[0].astype(F32)
    conv = cb_ref[...]
    for k in reversed(range(SSD_CONV)):
        lo = TAIL_ROWS - (SSD_CONV - 1) + k
        conv = conv + cw_ref[k:k + 1, :] * xpad_scr[lo:lo + t, :]
    xpad_scr[TAIL_ROWS - (SSD_CONV - 1):TAIL_ROWS, :] = xpad_scr[TAIL_ROWS + t - (SSD_CONV - 1):TAIL_ROWS + t, :]
    xc_all = _silu(conv)

    dt_all = jax.nn.softplus(dt_ref[0] + dtb_ref[...])
    a_all = dt_all * (-jnp.exp(alog_ref[...]))

    li = lax.broadcasted_iota(jnp.int32, (q, q), 0)
    si = lax.broadcasted_iota(jnp.int32, (q, q), 1)
    causal = li >= si

    for s in range(n_sub):
        rows = slice(s * q, (s + 1) * q)
        xs = xc_all[rows, :SSD_INNER]
        bm = xc_all[rows, SSD_INNER:SSD_INNER + SSD_G * SSD_N]
        cm = xc_all[rows, SSD_INNER + SSD_G * SSD_N:]
        dt = dt_all[rows]
        a_cs = _dot(lincl_ref[...], jnp.concatenate(_split_terms(a_all[rows], CUMSUM_TERMS), axis=0))
        a_last = a_cs[q - 1:q, :]
        a_cs_t = _dot_nt(eye_ref[...], jnp.concatenate(_split_terms(a_cs, CUMSUM_TERMS), axis=1))
        cols = jnp.concatenate([dt, jnp.exp(a_cs), dt * jnp.exp(a_last - a_cs)], axis=0)
        cols_x = _dot(jnp.concatenate(_split_terms(cols, EXPAND_TERMS), axis=1), expand_ref[...])
        dt_x, ea_x, dtdec_x = cols_x[0:q], cols_x[q:2 * q], cols_x[2 * q:3 * q]
        x_dt = (xs * dt_x).astype(BF16)
        x_dec = (xs * dtdec_x).astype(BF16)
        x_heads = [x_dt * hmask_ref[r] for r in range(SSD_R)]
        for g in range(SSD_G):
            gl = slice(g * SSD_GW, (g + 1) * SSD_GW)
            b_g = bm[:, g * SSD_N:(g + 1) * SSD_N].astype(BF16)
            c_g = cm[:, g * SSD_N:(g + 1) * SSD_N].astype(BF16)
            cbm = jnp.where(causal, _dot_nt(c_g, b_g), 0.0)
            m_cat = jnp.concatenate(
                [(cbm * jnp.exp(jnp.minimum(a_cs[:, h:h + 1] - a_cs_t[h:h + 1, :], 0.0))).astype(BF16)
                 for h in range(g * SSD_R, (g + 1) * SSD_R)], axis=1)
            for r in range(SSD_R):
                xbd_scr[r * q:(r + 1) * q, :] = x_heads[r][:, gl]
            st = st_scr[g]
            y = _dot(m_cat, xbd_scr[...]) + _dot(c_g, st.astype(BF16)) * ea_x[:, gl] + xs[:, gl] * dsk_ref[:, gl]
            st_scr[g] = st * ea_x[q - 1:q, gl] + _dot_tn(b_g, x_dec[:, gl])
            yg = y * _silu(z_ref[0, rows, gl].astype(F32))
            yg = yg * lax.rsqrt(jnp.mean(yg * yg, axis=-1, keepdims=True) + EPS)
            y_ref[0, rows, gl] = (yg * nw_ref[:, gl]).astype(y_ref.dtype)

    @pl.when(c == pl.num_programs(1) - 1)
    def _():
        stout_ref[0] = st_scr[...]
        tailout_ref[0] = xpad_scr[0:TAIL_ROWS, :]


def _ssd_call(z, xbc, dt, cw, cb, dtb, alog, dsk, nw, st0, tail0, q, n_sub):
    bsz, seq, _ = z.shape
    t = q * n_sub
    assert seq % t == 0 and t >= SSD_CONV - 1
    expand = np.zeros((LANES, SSD_INNER), np.float32)
    hmask = np.zeros((SSD_R, q, SSD_INNER), np.float32)
    for h in range(SSD_HEADS):
        expand[h, h * SSD_P:(h + 1) * SSD_P] = 1.0
        hmask[h % SSD_R, :, h * SSD_P:(h + 1) * SSD_P] = 1.0
    lincl = jnp.asarray(_stack(np.tril(np.ones((q, q), np.float32)), CUMSUM_TERMS, 1), BF16)
    eye = jnp.asarray(_stack(np.eye(LANES, dtype=np.float32), CUMSUM_TERMS, 1), BF16)
    expand = jnp.asarray(_stack(expand, EXPAND_TERMS, 0), BF16)
    hmask = jnp.asarray(hmask, BF16)
    blk = lambda b, c: (b, c, 0)
    per_b3 = lambda b, c: (b, 0, 0)
    per_b4 = lambda b, c: (b, 0, 0, 0)
    return pl.pallas_call(
        functools.partial(_ssd_body, q=q, n_sub=n_sub),
        grid=(bsz, seq // t),
        in_specs=[
            pl.BlockSpec((1, t, SSD_INNER), blk),
            pl.BlockSpec((1, t, SSD_XBC), blk),
            pl.BlockSpec((1, t, LANES), blk),
            _const_spec(cw.shape), _const_spec(cb.shape), _const_spec(dtb.shape), _const_spec(alog.shape),
            _const_spec(dsk.shape), _const_spec(nw.shape),
            _const_spec(lincl.shape), _const_spec(eye.shape), _const_spec(expand.shape), _const_spec(hmask.shape),
            _const_spec(st0.shape), _const_spec(tail0.shape),
        ],
        out_specs=[
            pl.BlockSpec((1, t, SSD_INNER), blk),
            pl.BlockSpec((1, SSD_G, SSD_N, SSD_GW), per_b4),
            pl.BlockSpec((1, TAIL_ROWS, SSD_XBC), per_b3),
        ],
        out_shape=[
            jax.ShapeDtypeStruct((bsz, seq, SSD_INNER), BF16),
            jax.ShapeDtypeStruct((bsz, SSD_G, SSD_N, SSD_GW), F32),
            jax.ShapeDtypeStruct((bsz, TAIL_ROWS, SSD_XBC), F32),
        ],
        scratch_shapes=[
            pltpu.VMEM((SSD_G, SSD_N, SSD_GW), F32),
            pltpu.VMEM((TAIL_ROWS + t, SSD_XBC), F32),
            pltpu.VMEM((SSD_R * q, SSD_GW), BF16),
        ],
        compiler_params=_params(("arbitrary", "arbitrary")),
        name="ssd",
    )(z, xbc, dt, cw, cb, dtb, alog, dsk, nw, lincl, eye, expand, hmask, st0, tail0)


def _hg_levels(q):
    out, m = [], q // 2
    while m >= 1:
        out.append(m)
        m //= 2
    return out


def _hg_sum_matrices(q):
    t = np.arange(q)
    l, tt = t[:, None], t[None, :]
    mats = [tt <= l, tt > l]
    for m in _hg_levels(q):
        ref = (l // (2 * m)) * (2 * m) + m - 1
        lower = (l % (2 * m)) >= m
        mats.append(np.where(lower, (tt > ref) & (tt <= l), (tt > l) & (tt <= ref)))
    return np.concatenate(mats, 0).astype(np.float32)


def _hg_level_masks(q):
    t = np.arange(q)
    l, s = t[:, None], t[None, :]
    masks = [(l > s) & (((l ^ s) >> int(np.log2(m))) == 1) for m in _hg_levels(q)] + [l == s]
    return np.stack([np.concatenate([mk, mk], 1) for mk in masks]).astype(np.float32)


def _block_diag_pair(x):
    z = jnp.zeros((x.shape[0], HG_K), x.dtype)
    return jnp.concatenate([jnp.concatenate([x[:, :HG_K], z], 1), jnp.concatenate([z, x[:, HG_K:]], 1)], 0)


def _hgrn2_body(q_ref, f_ref, i_ref, g_ref, lbp_ref, nw_ref, sums_ref, masks_ref, st0_ref,
                y_ref, stout_ref, st_scr, *, q, n_sub, layer):
    c = pl.program_id(1)

    @pl.when(c == 0)
    def _():
        st_scr[...] = st0_ref[...]

    lbp = lbp_ref[...]
    e = jnp.exp(lbp - jnp.max(lbp, axis=0, keepdims=True))
    lb = jnp.sum(e[:layer + 1], axis=0, keepdims=True) / jnp.sum(e, axis=0, keepdims=True)

    n_lev = len(_hg_levels(q))
    for s in range(n_sub):
        rows = slice(s * q, (s + 1) * q)
        f = lb + (1.0 - lb) * jax.nn.sigmoid(f_ref[0, rows, :].astype(F32))
        kk_b = (1.0 - f).astype(BF16)
        qs_b = _silu(q_ref[0, rows, :].astype(F32)).astype(BF16)
        v = i_ref[0, rows, :]
        lf = jnp.log(f) * np.float32(np.log2(np.e))
        p = jnp.exp2(_dot(sums_ref[...], jnp.concatenate(_split_terms(lf, EXPAND_TERMS), axis=0)))
        p_last = p[q - 1:q]
        p = p.astype(BF16)
        wq = [qs_b * p[(2 + j) * q:(3 + j) * q] for j in range(n_lev)] + [qs_b]
        wk = [kk_b * p[(2 + j) * q:(3 + j) * q] for j in range(n_lev)] + [kk_b]
        qg = qs_b * p[0:q]
        kd = kk_b * p[q:2 * q]

        for pair in range(HG_HEADS // 2):
            pl2 = slice(2 * pair * HG_K, 2 * (pair + 1) * HG_K)
            att = jnp.zeros((q, 2 * q), BF16)
            for j in range(n_lev + 1):
                a_j = _dot_nt(wq[j][:, pl2], _block_diag_pair(wk[j][:, pl2]))
                att = att + a_j.astype(BF16) * masks_ref[j]
            st = st_scr[pair]
            o = _dot(att, _block_diag_pair(v[:, pl2])) + _dot_nt(qg[:, pl2], st.astype(BF16))
            for hh in range(2):
                h = 2 * pair + hh
                hl = slice(h * HG_K, (h + 1) * HG_K)
                bl = slice(hh * HG_K, (hh + 1) * HG_K)
                st_scr[pair, bl, bl] = st[bl, bl] * p_last[:, hl] + _dot_tn(v[:, hl], kd[:, hl])
                o_h = o[:, bl]
                o_h = o_h * lax.rsqrt(jnp.mean(o_h * o_h, axis=-1, keepdims=True) + EPS) * nw_ref[:, hl]
                y_ref[0, rows, hl] = (o_h * _silu(g_ref[0, rows, hl].astype(F32))).astype(y_ref.dtype)

    @pl.when(c == pl.num_programs(1) - 1)
    def _():
        stout_ref[0] = st_scr[...]


def _hgrn2_call(qq, ff, ii, gg, lbp, nw, st0, q, n_sub, layer):
    bsz, seq, _ = qq.shape
    t = q * n_sub
    assert seq % t == 0 and q & (q - 1) == 0
    sums = jnp.asarray(_stack(_hg_sum_matrices(q), EXPAND_TERMS, 1), BF16)
    masks = jnp.asarray(_hg_level_masks(q), BF16)
    blk = lambda b, c: (b, c, 0)
    per_b4 = lambda b, c: (b, 0, 0, 0)
    tok = pl.BlockSpec((1, t, HG_WIDTH), blk)
    st_shape = (HG_HEADS // 2, 2 * HG_K, 2 * HG_K)
    return pl.pallas_call(
        functools.partial(_hgrn2_body, q=q, n_sub=n_sub, layer=layer),
        grid=(bsz, seq // t),
        in_specs=[tok, tok, tok, tok, _const_spec(lbp.shape), _const_spec(nw.shape),
                  _const_spec(sums.shape), _const_spec(masks.shape), _const_spec(st0.shape)],
        out_specs=[tok, pl.BlockSpec((1,) + st_shape, per_b4)],
        out_shape=[jax.ShapeDtypeStruct((bsz, seq, HG_WIDTH), BF16),
                   jax.ShapeDtypeStruct((bsz,) + st_shape, F32)],
        scratch_shapes=[pltpu.VMEM(st_shape, F32)],
        compiler_params=_params(("arbitrary", "arbitrary")),
        name="hgrn2",
    )(qq, ff, ii, gg, lbp, nw, sums, masks, st0)


def _tail_body(h_ref, ya_ref, yb_ref, ga_ref, gb_ref, wa_ref, wb_ref, wo_ref,
               nw_ref, wgu_ref, wd_ref, fnw_ref, o_ref, acc_ref):
    merged = (jax.nn.sigmoid(ga_ref[...].astype(F32)) * _dot(ya_ref[...], wa_ref[...])
              + jax.nn.sigmoid(gb_ref[...].astype(F32)) * _dot(yb_ref[...], wb_ref[...]))
    h = h_ref[...] + _dot(merged.astype(BF16), wo_ref[...])
    xn = _rms(h, nw_ref[...]).astype(BF16)
    _ffn_accumulate(xn, wgu_ref, wd_ref, acc_ref)
    h = h + acc_ref[...]
    o_ref[...] = _rms(h, fnw_ref[...])


def _tail_call(h, ya, yb, ga, gb, wa, wb, wo, nw, wgu, wd, fnw, tm):
    m = h.shape[0]
    row = lambda i: (i, 0)
    tile = pl.BlockSpec((tm, D_MODEL), row)
    return pl.pallas_call(
        _tail_body,
        grid=(m // tm,),
        in_specs=[tile, tile, tile, tile, tile,
                  _const_spec(wa.shape), _const_spec(wb.shape), _const_spec(wo.shape),
                  _const_spec((1, D_MODEL)),
                  _const_spec(wgu.shape), _const_spec(wd.shape),
                  _const_spec((1, D_MODEL))],
        out_specs=tile,
        out_shape=jax.ShapeDtypeStruct((m, D_MODEL), F32),
        scratch_shapes=[pltpu.VMEM((tm, D_MODEL), F32)],
        compiler_params=_params(("arbitrary",)),
        name="tail",
    )(h, ya, yb, ga, gb, wa, wb, wo, nw, wgu, wd, fnw)


def _ffn_weights(w_gu, w_down):
    return w_gu.astype(BF16), (0.5 * w_down).astype(BF16)


def _pad_lanes(v):
    return jnp.zeros((1, LANES), F32).at[0, :v.shape[0]].set(v.astype(F32))


class _Tiling(NamedTuple):
    rows: int
    chunk: int
    n_sub: int


MAIN_TILING = _Tiling(rows=512, chunk=128, n_sub=2)


def _mixers(u, bsz, seq, tiling, w_parts, ssd_p, hg_p, ssd_state, hg_state):
    outs = _inproj_call(u, w_parts, [BF16, BF16, F32, BF16, BF16, BF16, BF16, BF16, BF16], tiling.rows)
    z, xbc, dt, hq, hf, hi, hg, ga, gb = outs
    r3 = lambda t: t.reshape(bsz, seq, t.shape[-1])
    ya, st_ssd, tail = _ssd_call(r3(z), r3(xbc), r3(dt), *ssd_p, *ssd_state, q=tiling.chunk, n_sub=tiling.n_sub)
    yb, st_hg = _hgrn2_call(r3(hq), r3(hf), r3(hi), r3(hg), *hg_p, hg_state,
                            q=tiling.chunk, n_sub=tiling.n_sub, layer=0)
    return ya, yb, ga, gb, (st_ssd, tail), st_hg


def kernel(x, meta_tokens, ffn1_norm, ffn1_w_gu, ffn1_w_down, mix_norm, w_in, ssd_conv_w, ssd_conv_b,
           ssd_dt_bias, ssd_a_log, ssd_d, ssd_norm, hg_lower_bound, hg_norm, w_branch_a, w_branch_b,
           w_out, ffn2_norm, ffn2_w_gu, ffn2_w_down, final_norm):
    bsz, seq, d = x.shape
    assert d == D_MODEL and ffn1_norm.shape[0] == 1, "single-layer block of width D_MODEL only"
    n_meta = meta_tokens.shape[0]
    row = lambda v: v.reshape(1, -1).astype(F32)

    wgu1, wd1 = _ffn_weights(ffn1_w_gu[0], ffn1_w_down[0])
    wgu2, wd2 = _ffn_weights(ffn2_w_gu[0], ffn2_w_down[0])
    sizes = (SSD_INNER, SSD_XBC, SSD_HEADS, HG_WIDTH, HG_WIDTH, HG_WIDTH, HG_WIDTH, D_MODEL, D_MODEL)
    offs = np.cumsum((0,) + sizes)
    w_parts = [w_in[0][:, offs[j]:offs[j + 1]].astype(BF16) for j in range(len(sizes))]
    w_parts[2] = jnp.pad(w_parts[2], ((0, 0), (0, LANES - SSD_HEADS)))

    ssd_p = (ssd_conv_w[0].astype(F32), row(ssd_conv_b[0]), _pad_lanes(ssd_dt_bias[0]), _pad_lanes(ssd_a_log[0]),
             row(jnp.repeat(ssd_d[0], SSD_P)), row(ssd_norm[0]))
    hg_p = (hg_lower_bound.astype(F32), row(hg_norm[0]))

    meta_tiling = _Tiling(rows=n_meta, chunk=n_meta, n_sub=1)
    _, u_m = _ffn1_call(meta_tokens.astype(F32), row(ffn1_norm[0]), wgu1, wd1, row(mix_norm[0]), meta_tiling.rows)
    zero_ssd = (jnp.zeros((SSD_G, SSD_N, SSD_GW), F32), jnp.zeros((TAIL_ROWS, SSD_XBC), F32))
    zero_hg = jnp.zeros((HG_HEADS // 2, 2 * HG_K, 2 * HG_K), F32)
    _, _, _, _, (st_ssd, tail), st_hg = _mixers(u_m, 1, n_meta, meta_tiling, w_parts, ssd_p, hg_p,
                                                zero_ssd, zero_hg)

    tm = MAIN_TILING.rows
    x2 = x.reshape(bsz * seq, d)
    h1, u = _ffn1_call(x2, row(ffn1_norm[0]), wgu1, wd1, row(mix_norm[0]), tm)
    ya, yb, ga, gb, _, _ = _mixers(u, bsz, seq, MAIN_TILING, w_parts, ssd_p, hg_p,
                                   (st_ssd[0], tail[0]), st_hg[0])
    out = _tail_call(h1, ya.reshape(bsz * seq, -1), yb.reshape(bsz * seq, -1), ga, gb,
                     w_branch_a[0].astype(BF16), w_branch_b[0].astype(BF16), w_out[0].astype(BF16),
                     row(ffn2_norm[0]), wgu2, wd2, row(final_norm), tm)
    return out.reshape(bsz, seq, d)
```

```python
import functools
from typing import NamedTuple

import numpy as np
import jax
import jax.numpy as jnp
from jax import lax
from jax.experimental import pallas as pl
from jax.experimental.pallas import tpu as pltpu

F32 = jnp.float32
BF16 = jnp.bfloat16
EPS = 1e-6

D_MODEL = 1024
D_FF = 2816
FF_CHUNK = 256
N_FF_CHUNKS = D_FF // FF_CHUNK
PROJ_CHUNK = 256

SSD_HEADS = 16
SSD_P = 64
SSD_G = 4
SSD_R = SSD_HEADS // SSD_G
SSD_N = 128
SSD_INNER = SSD_HEADS * SSD_P
SSD_CONV = 4
SSD_XBC = SSD_INNER + 2 * SSD_G * SSD_N
SSD_GW = SSD_R * SSD_P
LANES = 128
TAIL_ROWS = 8
CUMSUM_TERMS = 3
EXPAND_TERMS = 2

HG_HEADS = 8
HG_K = 128
HG_WIDTH = HG_HEADS * HG_K

VMEM_LIMIT_BYTES = 56 * 1024 * 1024


def _params(sem):
    return pltpu.CompilerParams(dimension_semantics=sem, vmem_limit_bytes=VMEM_LIMIT_BYTES)


def _const_spec(shape):
    zeros = (0,) * len(shape)
    return pl.BlockSpec(shape, lambda *_: zeros, pipeline_mode=pl.Buffered(1))


def _rms(x, w):
    return x * lax.rsqrt(jnp.mean(x * x, axis=-1, keepdims=True) + EPS) * w


def _silu(x):
    return x * jax.nn.sigmoid(x)


def _split_terms(x, n):
    terms, rest = [], x
    for i in range(n):
        terms.append(rest.astype(BF16))
        if i + 1 < n:
            rest = rest - terms[-1].astype(F32)
    return terms


def _dot(a, b):
    return jnp.dot(a, b, preferred_element_type=F32)


def _dot_nt(a, b):
    return lax.dot_general(a, b, (((1,), (1,)), ((), ())), preferred_element_type=F32)


def _dot_tn(a, b):
    return lax.dot_general(a, b, (((0,), (0,)), ((), ())), preferred_element_type=F32)


def _stack(mat, n, axis):
    return np.concatenate([mat] * n, axis=axis)


def _merged(*stage_generators):
    live = list(stage_generators)
    while live:
        for gen in list(live):
            if next(gen, StopIteration) is StopIteration:
                live.remove(gen)
        yield


def _run(*stage_generators):
    for _ in _merged(*stage_generators):
        pass


def _ffn_stages(xn, wgu_ref, wd_ref, acc_ref):
    for c in range(N_FF_CHUNKS):
        g = _dot(xn, wgu_ref[:, c * FF_CHUNK:(c + 1) * FF_CHUNK])
        u = _dot(xn, wgu_ref[:, D_FF + c * FF_CHUNK:D_FF + (c + 1) * FF_CHUNK])
        d = _dot((_silu(g) * u).astype(BF16), wd_ref[c * FF_CHUNK:(c + 1) * FF_CHUNK, :])
        if c == 0:
            acc_ref[...] = d
        else:
            acc_ref[...] += d
        yield


def _ffn1_body(x_ref, nw_ref, wgu_ref, wd_ref, nw2_ref, h_ref, u_ref, acc_ref):
    x = x_ref[...]
    xn = _rms(x, nw_ref[...]).astype(BF16)
    _run(_ffn_stages(xn, wgu_ref, wd_ref, acc_ref))
    h = x + acc_ref[...]
    h_ref[...] = h
    u_ref[...] = _rms(h, nw2_ref[...]).astype(BF16)


def _ffn1_call(x2, nw, wgu, wd, nw2, tm):
    m = x2.shape[0]
    row = lambda i: (i, 0)
    return pl.pallas_call(
        _ffn1_body,
        grid=(m // tm,),
        in_specs=[
            pl.BlockSpec((tm, D_MODEL), row),
            _const_spec((1, D_MODEL)),
            _const_spec(wgu.shape), _const_spec(wd.shape),
            _const_spec((1, D_MODEL)),
        ],
        out_specs=[pl.BlockSpec((tm, D_MODEL), row), pl.BlockSpec((tm, D_MODEL), row)],
        out_shape=[jax.ShapeDtypeStruct((m, D_MODEL), F32), jax.ShapeDtypeStruct((m, D_MODEL), BF16)],
        scratch_shapes=[pltpu.VMEM((tm, D_MODEL), F32)],
        compiler_params=_params(("arbitrary",)),
        name="ffn1",
    )(x2, nw, wgu, wd, nw2)


def _proj_stages(u_ref, w_refs, o_refs):
    u = u_ref[...]
    for w_ref, o_ref in zip(w_refs, o_refs):
        width = w_ref.shape[1]
        step = min(PROJ_CHUNK, width)
        for lo in range(0, width, step):
            o_ref[:, lo:lo + step] = _dot(u, w_ref[:, lo:lo + step]).astype(o_ref.dtype)
            yield


def _inproj_body(u_ref, *refs):
    n = len(refs) // 2
    _run(_proj_stages(u_ref, refs[:n], refs[n:]))


def _inproj_call(u2, weights, out_dtypes, tm):
    m = u2.shape[0]
    row = lambda i: (i, 0)
    return pl.pallas_call(
        _inproj_body,
        grid=(m // tm,),
        in_specs=[pl.BlockSpec((tm, D_MODEL), row)] + [_const_spec(w.shape) for w in weights],
        out_specs=[pl.BlockSpec((tm, w.shape[1]), row) for w in weights],
        out_shape=[jax.ShapeDtypeStruct((m, w.shape[1]), dt) for w, dt in zip(weights, out_dtypes)],
        compiler_params=_params(("arbitrary",)),
        name="inproj",
    )(u2, *weights)


class _SsdConsts(NamedTuple):
    cw: jax.Array
    cb: jax.Array
    dtb: jax.Array
    alog: jax.Array
    dsk: jax.Array
    nw: jax.Array
    lincl: jax.Array
    eye: jax.Array
    expand: jax.Array
    hmask: jax.Array


def _ssd_consts(cw, cb, dtb, alog, dsk, nw, q):
    expand = np.zeros((LANES, SSD_INNER), np.float32)
    hmask = np.zeros((SSD_R, q, SSD_INNER), np.float32)
    for h in range(SSD_HEADS):
        expand[h, h * SSD_P:(h + 1) * SSD_P] = 1.0
        hmask[h % SSD_R, :, h * SSD_P:(h + 1) * SSD_P] = 1.0
    return _SsdConsts(
        cw, cb, dtb, alog, dsk, nw,
        jnp.asarray(_stack(np.tril(np.ones((q, q), np.float32)), CUMSUM_TERMS, 1), BF16),
        jnp.asarray(_stack(np.eye(LANES, dtype=np.float32), CUMSUM_TERMS, 1), BF16),
        jnp.asarray(_stack(expand, EXPAND_TERMS, 0), BF16),
        jnp.asarray(hmask, BF16))


def _ssd_stages(z_ref, xbc_ref, dt_ref, k, y_ref, st_scr, xpad_scr, xbd_scr, *, q, n_sub):
    t = n_sub * q
    xpad_scr[TAIL_ROWS:TAIL_ROWS + t, :] = xbc_ref[...].astype(F32)
    neg_a = -jnp.exp(k.alog[...])
    li = lax.broadcasted_iota(jnp.int32, (q, q), 0)
    si = lax.broadcasted_iota(jnp.int32, (q, q), 1)
    causal = li >= si

    for s in range(n_sub):
        rows = slice(s * q, (s + 1) * q)
        conv = k.cb[...]
        for tap in reversed(range(SSD_CONV)):
            lo = TAIL_ROWS - (SSD_CONV - 1) + tap + s * q
            conv = conv + k.cw[tap:tap + 1, :] * xpad_scr[lo:lo + q, :]
        xc = _silu(conv)
        xs = xc[:, :SSD_INNER]
        bm = xc[:, SSD_INNER:SSD_INNER + SSD_G * SSD_N]
        cm = xc[:, SSD_INNER + SSD_G * SSD_N:]
        dt = jax.nn.softplus(dt_ref[rows, :] + k.dtb[...])
        a_cs = _dot(k.lincl[...], jnp.concatenate(_split_terms(dt * neg_a, CUMSUM_TERMS), axis=0))
        a_last = a_cs[q - 1:q, :]
        a_cs_t = _dot_nt(k.eye[...], jnp.concatenate(_split_terms(a_cs, CUMSUM_TERMS), axis=1))
        cols = jnp.concatenate([dt, jnp.exp(a_cs), dt * jnp.exp(a_last - a_cs)], axis=0)
        cols_x = _dot(jnp.concatenate(_split_terms(cols, EXPAND_TERMS), axis=1), k.expand[...])
        dt_x, ea_x, dtdec_x = cols_x[0:q], cols_x[q:2 * q], cols_x[2 * q:3 * q]
        x_dt = (xs * dt_x).astype(BF16)
        x_dec = (xs * dtdec_x).astype(BF16)
        x_heads = [x_dt * k.hmask[r] for r in range(SSD_R)]
        yield
        for g in range(SSD_G):
            gl = slice(g * SSD_GW, (g + 1) * SSD_GW)
            b_g = bm[:, g * SSD_N:(g + 1) * SSD_N].astype(BF16)
            c_g = cm[:, g * SSD_N:(g + 1) * SSD_N].astype(BF16)
            cbm = jnp.where(causal, _dot_nt(c_g, b_g), 0.0)
            m_cat = jnp.concatenate(
                [(cbm * jnp.exp(jnp.minimum(a_cs[:, h:h + 1] - a_cs_t[h:h + 1, :], 0.0))).astype(BF16)
                 for h in range(g * SSD_R, (g + 1) * SSD_R)], axis=1)
            for r in range(SSD_R):
                xbd_scr[r * q:(r + 1) * q, :] = x_heads[r][:, gl]
            st = st_scr[g]
            y = _dot(m_cat, xbd_scr[...]) + _dot(c_g, st.astype(BF16)) * ea_x[:, gl] + xs[:, gl] * k.dsk[:, gl]
            st_scr[g] = st * ea_x[q - 1:q, gl] + _dot_tn(b_g, x_dec[:, gl])
            yg = y * _silu(z_ref[rows, gl].astype(F32))
            yg = yg * lax.rsqrt(jnp.mean(yg * yg, axis=-1, keepdims=True) + EPS)
            y_ref[rows, gl] = (yg * k.nw[:, gl]).astype(y_ref.dtype)
            yield
    xpad_scr[TAIL_ROWS - (SSD_CONV - 1):TAIL_ROWS, :] = xpad_scr[TAIL_ROWS + t - (SSD_CONV - 1):TAIL_ROWS + t, :]
    yield


N_SSD_CONSTS = len(_SsdConsts._fields)


def _ssd_body(z_ref, xbc_ref, dt_ref, *refs, q, n_sub):
    k = _SsdConsts(*refs[:N_SSD_CONSTS])
    st0_ref, tail0_ref, y_ref, stout_ref, tailout_ref, st_scr, xpad_scr, xbd_scr = refs[N_SSD_CONSTS:]
    c = pl.program_id(1)

    @pl.when(c == 0)
    def _():
        st_scr[...] = st0_ref[...]
        xpad_scr[0:TAIL_ROWS, :] = tail0_ref[...]

    _run(_ssd_stages(z_ref.at[0], xbc_ref.at[0], dt_ref.at[0], k, y_ref.at[0], st_scr, xpad_scr, xbd_scr,
                     q=q, n_sub=n_sub))

    @pl.when(c == pl.num_programs(1) - 1)
    def _():
        stout_ref[0] = st_scr[...]
        tailout_ref[0] = xpad_scr[0:TAIL_ROWS, :]


def _ssd_scratch(q, t):
    return [pltpu.VMEM((SSD_G, SSD_N, SSD_GW), F32),
            pltpu.VMEM((TAIL_ROWS + t, SSD_XBC), F32),
            pltpu.VMEM((SSD_R * q, SSD_GW), BF16)]


def _ssd_call(z, xbc, dt, consts, st0, tail0, q, n_sub):
    bsz, seq, _ = z.shape
    t = q * n_sub
    assert seq % t == 0 and t >= SSD_CONV - 1
    blk = lambda b, c: (b, c, 0)
    per_b3 = lambda b, c: (b, 0, 0)
    per_b4 = lambda b, c: (b, 0, 0, 0)
    return pl.pallas_call(
        functools.partial(_ssd_body, q=q, n_sub=n_sub),
        grid=(bsz, seq // t),
        in_specs=[
            pl.BlockSpec((1, t, SSD_INNER), blk),
            pl.BlockSpec((1, t, SSD_XBC), blk),
            pl.BlockSpec((1, t, LANES), blk),
        ] + [_const_spec(c.shape) for c in consts] + [_const_spec(st0.shape), _const_spec(tail0.shape)],
        out_specs=[
            pl.BlockSpec((1, t, SSD_INNER), blk),
            pl.BlockSpec((1, SSD_G, SSD_N, SSD_GW), per_b4),
            pl.BlockSpec((1, TAIL_ROWS, SSD_XBC), per_b3),
        ],
        out_shape=[
            jax.ShapeDtypeStruct((bsz, seq, SSD_INNER), BF16),
            jax.ShapeDtypeStruct((bsz, SSD_G, SSD_N, SSD_GW), F32),
            jax.ShapeDtypeStruct((bsz, TAIL_ROWS, SSD_XBC), F32),
        ],
        scratch_shapes=_ssd_scratch(q, t),
        compiler_params=_params(("arbitrary", "arbitrary")),
        name="ssd",
    )(z, xbc, dt, *consts, st0, tail0)


def _projssd_body(u_ref, wz_ref, wx_ref, wdt_ref, *refs, q, n_sub, blocks_per_seq):
    k = _SsdConsts(*refs[:N_SSD_CONSTS])
    st0_ref, tail0_ref, y_ref, z_scr, xbc_scr, dt_scr, st_scr, xpad_scr, xbd_scr = refs[N_SSD_CONSTS:]
    i = pl.program_id(0)

    @pl.when((i == 0) | ((i + blocks_per_seq - 1) % blocks_per_seq == 0))
    def _():
        st_scr[...] = st0_ref[...]
        xpad_scr[0:TAIL_ROWS, :] = tail0_ref[...]

    @pl.when(i == 0)
    def _():
        for scr in (z_scr, xbc_scr, dt_scr):
            scr[1] = jnp.zeros(scr.shape[1:], scr.dtype)

    slot = i % 2
    _run(_ssd_stages(z_scr.at[1 - slot], xbc_scr.at[1 - slot], dt_scr.at[1 - slot], k, y_ref,
                     st_scr, xpad_scr, xbd_scr, q=q, n_sub=n_sub),
         _proj_stages(u_ref, (wz_ref, wx_ref, wdt_ref), (z_scr.at[slot], xbc_scr.at[slot], dt_scr.at[slot])))


def _projssd_call(u2, wz, wx, wdt, consts, st0, tail0, bsz, seq, q, n_sub):
    t = q * n_sub
    assert seq % t == 0 and t >= SSD_CONV - 1
    blocks_per_seq = seq // t
    n_blocks = bsz * blocks_per_seq
    cur = lambda i: (jnp.minimum(i, n_blocks - 1), 0)
    prev = lambda i: (jnp.maximum(i - 1, 0), 0)
    fixed = (wz, wx, wdt) + tuple(consts) + (st0, tail0)
    return pl.pallas_call(
        functools.partial(_projssd_body, q=q, n_sub=n_sub, blocks_per_seq=blocks_per_seq),
        grid=(n_blocks + 1,),
        in_specs=[pl.BlockSpec((t, D_MODEL), cur)] + [_const_spec(c.shape) for c in fixed],
        out_specs=pl.BlockSpec((t, SSD_INNER), prev),
        out_shape=jax.ShapeDtypeStruct((bsz * seq, SSD_INNER), BF16),
        scratch_shapes=[pltpu.VMEM((2, t, SSD_INNER), BF16), pltpu.VMEM((2, t, SSD_XBC), BF16),
                        pltpu.VMEM((2, t, LANES), F32)] + _ssd_scratch(q, t),
        compiler_params=_params(("arbitrary",)),
        name="proj_ssd",
    )(u2, *fixed)


def _hg_levels(q):
    out, m = [], q // 2
    while m >= 1:
        out.append(m)
        m //= 2
    return out


def _hg_sum_matrices(q):
    t = np.arange(q)
    l, tt = t[:, None], t[None, :]
    mats = [tt <= l, tt > l]
    for m in _hg_levels(q):
        ref = (l // (2 * m)) * (2 * m) + m - 1
        lower = (l % (2 * m)) >= m
        mats.append(np.where(lower, (tt > ref) & (tt <= l), (tt > l) & (tt <= ref)))
    return np.concatenate(mats, 0).astype(np.float32)


def _hg_level_masks(q):
    t = np.arange(q)
    l, s = t[:, None], t[None, :]
    masks = [(l > s) & (((l ^ s) >> int(np.log2(m))) == 1) for m in _hg_levels(q)] + [l == s]
    return np.stack([np.concatenate([mk, mk], 1) for mk in masks]).astype(np.float32)


def _block_diag_pair(x):
    z = jnp.zeros((x.shape[0], HG_K), x.dtype)
    return jnp.concatenate([jnp.concatenate([x[:, :HG_K], z], 1), jnp.concatenate([z, x[:, HG_K:]], 1)], 0)


def _hgrn2_stages(q_ref, f_ref, i_ref, g_ref, lbp_ref, nw_ref, sums_ref, masks_ref, y_ref, st_scr,
                  *, q, n_sub, layer):
    lbp = lbp_ref[...]
    e = jnp.exp(lbp - jnp.max(lbp, axis=0, keepdims=True))
    lb = jnp.sum(e[:layer + 1], axis=0, keepdims=True) / jnp.sum(e, axis=0, keepdims=True)

    n_lev = len(_hg_levels(q))

    def chunk(s):
        rows = slice(s * q, (s + 1) * q)
        f = lb + (1.0 - lb) * jax.nn.sigmoid(f_ref[rows, :].astype(F32))
        kk_b = (1.0 - f).astype(BF16)
        qs_b = _silu(q_ref[rows, :].astype(F32)).astype(BF16)
        v = i_ref[rows, :]
        lf = jnp.log(f) * np.float32(np.log2(np.e))
        p = jnp.exp2(_dot(sums_ref[...], jnp.concatenate(_split_terms(lf, EXPAND_TERMS), axis=0)))
        p_last = p[q - 1:q]
        p = p.astype(BF16)
        wq = [qs_b * p[(2 + j) * q:(3 + j) * q] for j in range(n_lev)] + [qs_b]
        wk = [kk_b * p[(2 + j) * q:(3 + j) * q] for j in range(n_lev)] + [kk_b]
        qg = qs_b * p[0:q]
        kd = kk_b * p[q:2 * q]
        yield

        for pair in range(HG_HEADS // 2):
            pl2 = slice(2 * pair * HG_K, 2 * (pair + 1) * HG_K)
            att = jnp.zeros((q, 2 * q), BF16)
            for j in range(n_lev + 1):
                a_j = _dot_nt(wq[j][:, pl2], _block_diag_pair(wk[j][:, pl2]))
                att = att + a_j.astype(BF16) * masks_ref[j]
            st = st_scr[pair]
            o = _dot(att, _block_diag_pair(v[:, pl2])) + _dot_nt(qg[:, pl2], st.astype(BF16))
            for hh in range(2):
                h = 2 * pair + hh
                hl = slice(h * HG_K, (h + 1) * HG_K)
                bl = slice(hh * HG_K, (hh + 1) * HG_K)
                st_scr[pair, bl, bl] = st[bl, bl] * p_last[:, hl] + _dot_tn(v[:, hl], kd[:, hl])
                o_h = o[:, bl]
                o_h = o_h * lax.rsqrt(jnp.mean(o_h * o_h, axis=-1, keepdims=True) + EPS) * nw_ref[:, hl]
                y_ref[rows, hl] = (o_h * _silu(g_ref[rows, hl].astype(F32))).astype(y_ref.dtype)
            yield

    for s in range(n_sub):
        yield from chunk(s)


def _hgrn2_body(q_ref, f_ref, i_ref, g_ref, lbp_ref, nw_ref, sums_ref, masks_ref, st0_ref,
                y_ref, stout_ref, st_scr, *, q, n_sub, layer):
    c = pl.program_id(1)

    @pl.when(c == 0)
    def _():
        st_scr[...] = st0_ref[...]

    _run(_hgrn2_stages(q_ref.at[0], f_ref.at[0], i_ref.at[0], g_ref.at[0], lbp_ref, nw_ref, sums_ref, masks_ref,
                       y_ref.at[0], st_scr, q=q, n_sub=n_sub, layer=layer))

    @pl.when(c == pl.num_programs(1) - 1)
    def _():
        stout_ref[0] = st_scr[...]


def _hgrn2_call(qq, ff, ii, gg, lbp, nw, st0, q, n_sub, layer):
    bsz, seq, _ = qq.shape
    t = q * n_sub
    assert seq % t == 0 and q & (q - 1) == 0
    sums = jnp.asarray(_stack(_hg_sum_matrices(q), EXPAND_TERMS, 1), BF16)
    masks = jnp.asarray(_hg_level_masks(q), BF16)
    blk = lambda b, c: (b, c, 0)
    per_b4 = lambda b, c: (b, 0, 0, 0)
    tok = pl.BlockSpec((1, t, HG_WIDTH), blk)
    st_shape = (HG_HEADS // 2, 2 * HG_K, 2 * HG_K)
    return pl.pallas_call(
        functools.partial(_hgrn2_body, q=q, n_sub=n_sub, layer=layer),
        grid=(bsz, seq // t),
        in_specs=[tok, tok, tok, tok, _const_spec(lbp.shape), _const_spec(nw.shape),
                  _const_spec(sums.shape), _const_spec(masks.shape), _const_spec(st0.shape)],
        out_specs=[tok, pl.BlockSpec((1,) + st_shape, per_b4)],
        out_shape=[jax.ShapeDtypeStruct((bsz, seq, HG_WIDTH), BF16),
                   jax.ShapeDtypeStruct((bsz,) + st_shape, F32)],
        scratch_shapes=[pltpu.VMEM(st_shape, F32)],
        compiler_params=_params(("arbitrary", "arbitrary")),
        name="hgrn2",
    )(qq, ff, ii, gg, lbp, nw, sums, masks, st0)


def _tail_body(h_ref, ya_ref, yb_ref, ga_ref, gb_ref, wa_ref, wb_ref, wo_ref,
               nw_ref, wgu_ref, wd_ref, fnw_ref, o_ref, acc_ref):
    merged = (jax.nn.sigmoid(ga_ref[...].astype(F32)) * _dot(ya_ref[...], wa_ref[...])
              + jax.nn.sigmoid(gb_ref[...].astype(F32)) * _dot(yb_ref[...], wb_ref[...]))
    h = h_ref[...] + _dot(merged.astype(BF16), wo_ref[...])
    xn = _rms(h, nw_ref[...]).astype(BF16)
    _run(_ffn_stages(xn, wgu_ref, wd_ref, acc_ref))
    h = h + acc_ref[...]
    o_ref[...] = _rms(h, fnw_ref[...])


def _tail_call(h, ya, yb, ga, gb, wa, wb, wo, nw, wgu, wd, fnw, tm):
    m = h.shape[0]
    row = lambda i: (i, 0)
    tile = pl.BlockSpec((tm, D_MODEL), row)
    return pl.pallas_call(
        _tail_body,
        grid=(m // tm,),
        in_specs=[tile, tile, tile, tile, tile,
                  _const_spec(wa.shape), _const_spec(wb.shape), _const_spec(wo.shape),
                  _const_spec((1, D_MODEL)),
                  _const_spec(wgu.shape), _const_spec(wd.shape),
                  _const_spec((1, D_MODEL))],
        out_specs=tile,
        out_shape=jax.ShapeDtypeStruct((m, D_MODEL), F32),
        scratch_shapes=[pltpu.VMEM((tm, D_MODEL), F32)],
        compiler_params=_params(("arbitrary",)),
        name="tail",
    )(h, ya, yb, ga, gb, wa, wb, wo, nw, wgu, wd, fnw)


def _ffn_weights(w_gu, w_down):
    return w_gu.astype(BF16), (0.5 * w_down).astype(BF16)


def _pad_lanes(v):
    return jnp.zeros((1, LANES), F32).at[0, :v.shape[0]].set(v.astype(F32))


class _Tiling(NamedTuple):
    rows: int
    chunk: int
    ssd_sub: int
    hg_sub: int


MAIN_TILING = _Tiling(rows=512, chunk=128, ssd_sub=4, hg_sub=2)


def kernel(x, meta_tokens, ffn1_norm, ffn1_w_gu, ffn1_w_down, mix_norm, w_in, ssd_conv_w, ssd_conv_b,
           ssd_dt_bias, ssd_a_log, ssd_d, ssd_norm, hg_lower_bound, hg_norm, w_branch_a, w_branch_b,
           w_out, ffn2_norm, ffn2_w_gu, ffn2_w_down, final_norm):
    bsz, seq, d = x.shape
    assert d == D_MODEL and ffn1_norm.shape[0] == 1, "single-layer block of width D_MODEL only"
    n_meta = meta_tokens.shape[0]
    row = lambda v: v.reshape(1, -1).astype(F32)
    r3 = lambda t, b, s: t.reshape(b, s, t.shape[-1])

    wgu1, wd1 = _ffn_weights(ffn1_w_gu[0], ffn1_w_down[0])
    wgu2, wd2 = _ffn_weights(ffn2_w_gu[0], ffn2_w_down[0])
    sizes = (SSD_INNER, SSD_XBC, SSD_HEADS, HG_WIDTH, HG_WIDTH, HG_WIDTH, HG_WIDTH, D_MODEL, D_MODEL)
    offs = np.cumsum((0,) + sizes)
    w_parts = [w_in[0][:, offs[j]:offs[j + 1]].astype(BF16) for j in range(len(sizes))]
    w_parts[2] = jnp.pad(w_parts[2], ((0, 0), (0, LANES - SSD_HEADS)))
    w_ssd, w_rest = w_parts[:3], w_parts[3:]
    ssd_dtypes, rest_dtypes = [BF16, BF16, F32], [BF16] * 6

    ssd_p = (ssd_conv_w[0].astype(F32), row(ssd_conv_b[0]), _pad_lanes(ssd_dt_bias[0]), _pad_lanes(ssd_a_log[0]),
             row(jnp.repeat(ssd_d[0], SSD_P)), row(ssd_norm[0]))
    hg_p = (hg_lower_bound.astype(F32), row(hg_norm[0]))

    _, u_m = _ffn1_call(meta_tokens.astype(F32), row(ffn1_norm[0]), wgu1, wd1, row(mix_norm[0]), n_meta)
    proj_m = _inproj_call(u_m, w_parts, ssd_dtypes + rest_dtypes, n_meta)
    zero_ssd = (jnp.zeros((SSD_G, SSD_N, SSD_GW), F32), jnp.zeros((TAIL_ROWS, SSD_XBC), F32))
    zero_hg = jnp.zeros((HG_HEADS // 2, 2 * HG_K, 2 * HG_K), F32)
    _, st_ssd, tail = _ssd_call(*(r3(t, 1, n_meta) for t in proj_m[:3]), _ssd_consts(*ssd_p, n_meta), *zero_ssd,
                                q=n_meta, n_sub=1)
    _, st_hg = _hgrn2_call(*(r3(t, 1, n_meta) for t in proj_m[3:7]), *hg_p, zero_hg, q=n_meta, n_sub=1, layer=0)

    tl = MAIN_TILING
    x2 = x.reshape(bsz * seq, d)
    h1, u = _ffn1_call(x2, row(ffn1_norm[0]), wgu1, wd1, row(mix_norm[0]), tl.rows)
    ya = _projssd_call(u, *w_ssd, _ssd_consts(*ssd_p, tl.chunk), st_ssd[0], tail[0], bsz, seq,
                       q=tl.chunk, n_sub=tl.ssd_sub)
    hq, hf, hi, hg, ga, gb = _inproj_call(u, w_rest, rest_dtypes, tl.rows)
    yb, _ = _hgrn2_call(*(r3(t, bsz, seq) for t in (hq, hf, hi, hg)), *hg_p, st_hg[0],
                        q=tl.chunk, n_sub=tl.hg_sub, layer=0)
    out = _tail_call(h1, ya, yb.reshape(bsz * seq, -1), ga, gb,
                     w_branch_a[0].astype(BF16), w_branch_b[0].astype(BF16), w_out[0].astype(BF16),
                     row(ffn2_norm[0]), wgu2, wd2, row(final_norm), tl.rows)
    return out.reshape(bsz, seq, d)
```

```python
import functools
from typing import NamedTuple

import numpy as np
import jax
import jax.numpy as jnp
from jax import lax
from jax.experimental import pallas as pl
from jax.experimental.pallas import tpu as pltpu

F32 = jnp.float32
BF16 = jnp.bfloat16
EPS = 1e-6

D_MODEL = 1024
D_FF = 2816
FF_CHUNK = 256
N_FF_CHUNKS = D_FF // FF_CHUNK
PROJ_CHUNK = 256

SSD_HEADS = 16
SSD_P = 64
SSD_G = 4
SSD_R = SSD_HEADS // SSD_G
SSD_N = 128
SSD_INNER = SSD_HEADS * SSD_P
SSD_CONV = 4
SSD_XBC = SSD_INNER + 2 * SSD_G * SSD_N
SSD_GW = SSD_R * SSD_P
LANES = 128
TAIL_ROWS = 8
CUMSUM_TERMS = 3
EXPAND_TERMS = 2

HG_HEADS = 8
HG_K = 128
HG_WIDTH = HG_HEADS * HG_K

VMEM_LIMIT_BYTES = 56 * 1024 * 1024


def _params(sem):
    return pltpu.CompilerParams(dimension_semantics=sem, vmem_limit_bytes=VMEM_LIMIT_BYTES)


def _const_spec(shape):
    zeros = (0,) * len(shape)
    return pl.BlockSpec(shape, lambda *_: zeros, pipeline_mode=pl.Buffered(1))


def _rms(x, w):
    return x * lax.rsqrt(jnp.mean(x * x, axis=-1, keepdims=True) + EPS) * w


def _silu(x):
    return x * jax.nn.sigmoid(x)


def _split_terms(x, n):
    terms, rest = [], x
    for i in range(n):
        terms.append(rest.astype(BF16))
        if i + 1 < n:
            rest = rest - terms[-1].astype(F32)
    return terms


def _dot(a, b):
    return jnp.dot(a, b, preferred_element_type=F32)


def _dot_nt(a, b):
    return lax.dot_general(a, b, (((1,), (1,)), ((), ())), preferred_element_type=F32)


def _dot_tn(a, b):
    return lax.dot_general(a, b, (((0,), (0,)), ((), ())), preferred_element_type=F32)


def _stack(mat, n, axis):
    return np.concatenate([mat] * n, axis=axis)


def _merged(*stage_generators):
    live = list(stage_generators)
    while live:
        for gen in list(live):
            if next(gen, StopIteration) is StopIteration:
                live.remove(gen)
        yield


def _run(*stage_generators):
    for _ in _merged(*stage_generators):
        pass


def _ffn_stages(xn, wgu_ref, wd_ref, acc_ref):
    for c in range(N_FF_CHUNKS):
        g = _dot(xn, wgu_ref[:, c * FF_CHUNK:(c + 1) * FF_CHUNK])
        u = _dot(xn, wgu_ref[:, D_FF + c * FF_CHUNK:D_FF + (c + 1) * FF_CHUNK])
        d = _dot((_silu(g) * u).astype(BF16), wd_ref[c * FF_CHUNK:(c + 1) * FF_CHUNK, :])
        if c == 0:
            acc_ref[...] = d
        else:
            acc_ref[...] += d
        yield


def _ffn1_body(x_ref, nw_ref, wgu_ref, wd_ref, nw2_ref, h_ref, u_ref, acc_ref):
    x = x_ref[...]
    xn = _rms(x, nw_ref[...]).astype(BF16)
    _run(_ffn_stages(xn, wgu_ref, wd_ref, acc_ref))
    h = x + acc_ref[...]
    h_ref[...] = h
    u_ref[...] = _rms(h, nw2_ref[...]).astype(BF16)


def _ffn1_call(x2, nw, wgu, wd, nw2, tm):
    m = x2.shape[0]
    row = lambda i: (i, 0)
    return pl.pallas_call(
        _ffn1_body,
        grid=(m // tm,),
        in_specs=[
            pl.BlockSpec((tm, D_MODEL), row),
            _const_spec((1, D_MODEL)),
            _const_spec(wgu.shape), _const_spec(wd.shape),
            _const_spec((1, D_MODEL)),
        ],
        out_specs=[pl.BlockSpec((tm, D_MODEL), row), pl.BlockSpec((tm, D_MODEL), row)],
        out_shape=[jax.ShapeDtypeStruct((m, D_MODEL), F32), jax.ShapeDtypeStruct((m, D_MODEL), BF16)],
        scratch_shapes=[pltpu.VMEM((tm, D_MODEL), F32)],
        compiler_params=_params(("arbitrary",)),
        name="ffn1",
    )(x2, nw, wgu, wd, nw2)


def _proj_stages(u_ref, w_refs, o_refs):
    u = u_ref[...]
    for w_ref, o_ref in zip(w_refs, o_refs):
        width = w_ref.shape[1]
        step = min(PROJ_CHUNK, width)
        for lo in range(0, width, step):
            o_ref[:, lo:lo + step] = _dot(u, w_ref[:, lo:lo + step]).astype(o_ref.dtype)
            yield


def _inproj_body(u_ref, *refs):
    n = len(refs) // 2
    _run(_proj_stages(u_ref, refs[:n], refs[n:]))


def _inproj_call(u2, weights, out_dtypes, tm):
    m = u2.shape[0]
    row = lambda i: (i, 0)
    return pl.pallas_call(
        _inproj_body,
        grid=(m // tm,),
        in_specs=[pl.BlockSpec((tm, D_MODEL), row)] + [_const_spec(w.shape) for w in weights],
        out_specs=[pl.BlockSpec((tm, w.shape[1]), row) for w in weights],
        out_shape=[jax.ShapeDtypeStruct((m, w.shape[1]), dt) for w, dt in zip(weights, out_dtypes)],
        compiler_params=_params(("arbitrary",)),
        name="inproj",
    )(u2, *weights)


class _SsdConsts(NamedTuple):
    cw: jax.Array
    cb: jax.Array
    dtb: jax.Array
    alog: jax.Array
    dsk: jax.Array
    nw: jax.Array
    lincl: jax.Array
    eye: jax.Array
    expand: jax.Array
    hmask: jax.Array


def _ssd_consts(cw, cb, dtb, alog, dsk, nw, q):
    expand = np.zeros((LANES, SSD_INNER), np.float32)
    hmask = np.zeros((SSD_R, q, SSD_INNER), np.float32)
    for h in range(SSD_HEADS):
        expand[h, h * SSD_P:(h + 1) * SSD_P] = 1.0
        hmask[h % SSD_R, :, h * SSD_P:(h + 1) * SSD_P] = 1.0
    return _SsdConsts(
        cw, cb, dtb, alog, dsk, nw,
        jnp.asarray(_stack(np.tril(np.ones((q, q), np.float32)), CUMSUM_TERMS, 1), BF16),
        jnp.asarray(_stack(np.eye(LANES, dtype=np.float32), CUMSUM_TERMS, 1), BF16),
        jnp.asarray(_stack(expand, EXPAND_TERMS, 0), BF16),
        jnp.asarray(hmask, BF16))


def _ssd_stages(z_ref, xbc_ref, dt_ref, k, y_ref, st_scr, xpad_scr, xbd_scr, *, q, n_sub):
    t = n_sub * q
    xpad_scr[TAIL_ROWS:TAIL_ROWS + t, :] = xbc_ref[...].astype(F32)
    neg_a = -jnp.exp(k.alog[...])
    li = lax.broadcasted_iota(jnp.int32, (q, q), 0)
    si = lax.broadcasted_iota(jnp.int32, (q, q), 1)
    causal = li >= si

    for s in range(n_sub):
        rows = slice(s * q, (s + 1) * q)
        conv = k.cb[...]
        for tap in reversed(range(SSD_CONV)):
            lo = TAIL_ROWS - (SSD_CONV - 1) + tap + s * q
            conv = conv + k.cw[tap:tap + 1, :] * xpad_scr[lo:lo + q, :]
        xc = _silu(conv)
        xs = xc[:, :SSD_INNER]
        bm = xc[:, SSD_INNER:SSD_INNER + SSD_G * SSD_N]
        cm = xc[:, SSD_INNER + SSD_G * SSD_N:]
        dt = jax.nn.softplus(dt_ref[rows, :] + k.dtb[...])
        a_cs = _dot(k.lincl[...], jnp.concatenate(_split_terms(dt * neg_a, CUMSUM_TERMS), axis=0))
        a_last = a_cs[q - 1:q, :]
        a_cs_t = _dot_nt(k.eye[...], jnp.concatenate(_split_terms(a_cs, CUMSUM_TERMS), axis=1))
        cols = jnp.concatenate([dt, jnp.exp(a_cs), dt * jnp.exp(a_last - a_cs)], axis=0)
        cols_x = _dot(jnp.concatenate(_split_terms(cols, EXPAND_TERMS), axis=1), k.expand[...])
        dt_x, ea_x, dtdec_x = cols_x[0:q], cols_x[q:2 * q], cols_x[2 * q:3 * q]
        x_dt = (xs * dt_x).astype(BF16)
        x_dec = (xs * dtdec_x).astype(BF16)
        x_heads = [x_dt * k.hmask[r] for r in range(SSD_R)]
        yield
        for g in range(SSD_G):
            gl = slice(g * SSD_GW, (g + 1) * SSD_GW)
            b_g = bm[:, g * SSD_N:(g + 1) * SSD_N].astype(BF16)
            c_g = cm[:, g * SSD_N:(g + 1) * SSD_N].astype(BF16)
            cbm = jnp.where(causal, _dot_nt(c_g, b_g), 0.0)
            m_cat = jnp.concatenate(
                [(cbm * jnp.exp(jnp.minimum(a_cs[:, h:h + 1] - a_cs_t[h:h + 1, :], 0.0))).astype(BF16)
                 for h in range(g * SSD_R, (g + 1) * SSD_R)], axis=1)
            for r in range(SSD_R):
                xbd_scr[r * q:(r + 1) * q, :] = x_heads[r][:, gl]
            st = st_scr[g]
            y = _dot(m_cat, xbd_scr[...]) + _dot(c_g, st.astype(BF16)) * ea_x[:, gl] + xs[:, gl] * k.dsk[:, gl]
            st_scr[g] = st * ea_x[q - 1:q, gl] + _dot_tn(b_g, x_dec[:, gl])
            yg = y * _silu(z_ref[rows, gl].astype(F32))
            yg = yg * lax.rsqrt(jnp.mean(yg * yg, axis=-1, keepdims=True) + EPS)
            y_ref[rows, gl] = (yg * k.nw[:, gl]).astype(y_ref.dtype)
            yield
    xpad_scr[TAIL_ROWS - (SSD_CONV - 1):TAIL_ROWS, :] = xpad_scr[TAIL_ROWS + t - (SSD_CONV - 1):TAIL_ROWS + t, :]
    yield


N_SSD_CONSTS = len(_SsdConsts._fields)


def _ssd_body(z_ref, xbc_ref, dt_ref, *refs, q, n_sub):
    k = _SsdConsts(*refs[:N_SSD_CONSTS])
    st0_ref, tail0_ref, y_ref, stout_ref, tailout_ref, st_scr, xpad_scr, xbd_scr = refs[N_SSD_CONSTS:]
    c = pl.program_id(1)

    @pl.when(c == 0)
    def _():
        st_scr[...] = st0_ref[...]
        xpad_scr[0:TAIL_ROWS, :] = tail0_ref[...]

    _run(_ssd_stages(z_ref.at[0], xbc_ref.at[0], dt_ref.at[0], k, y_ref.at[0], st_scr, xpad_scr, xbd_scr,
                     q=q, n_sub=n_sub))

    @pl.when(c == pl.num_programs(1) - 1)
    def _():
        stout_ref[0] = st_scr[...]
        tailout_ref[0] = xpad_scr[0:TAIL_ROWS, :]


def _ssd_scratch(q, t):
    return [pltpu.VMEM((SSD_G, SSD_N, SSD_GW), F32),
            pltpu.VMEM((TAIL_ROWS + t, SSD_XBC), F32),
            pltpu.VMEM((SSD_R * q, SSD_GW), BF16)]


def _ssd_call(z, xbc, dt, consts, st0, tail0, q, n_sub):
    bsz, seq, _ = z.shape
    t = q * n_sub
    assert seq % t == 0 and t >= SSD_CONV - 1
    blk = lambda b, c: (b, c, 0)
    per_b3 = lambda b, c: (b, 0, 0)
    per_b4 = lambda b, c: (b, 0, 0, 0)
    return pl.pallas_call(
        functools.partial(_ssd_body, q=q, n_sub=n_sub),
        grid=(bsz, seq // t),
        in_specs=[
            pl.BlockSpec((1, t, SSD_INNER), blk),
            pl.BlockSpec((1, t, SSD_XBC), blk),
            pl.BlockSpec((1, t, LANES), blk),
        ] + [_const_spec(c.shape) for c in consts] + [_const_spec(st0.shape), _const_spec(tail0.shape)],
        out_specs=[
            pl.BlockSpec((1, t, SSD_INNER), blk),
            pl.BlockSpec((1, SSD_G, SSD_N, SSD_GW), per_b4),
            pl.BlockSpec((1, TAIL_ROWS, SSD_XBC), per_b3),
        ],
        out_shape=[
            jax.ShapeDtypeStruct((bsz, seq, SSD_INNER), BF16),
            jax.ShapeDtypeStruct((bsz, SSD_G, SSD_N, SSD_GW), F32),
            jax.ShapeDtypeStruct((bsz, TAIL_ROWS, SSD_XBC), F32),
        ],
        scratch_shapes=_ssd_scratch(q, t),
        compiler_params=_params(("arbitrary", "arbitrary")),
        name="ssd",
    )(z, xbc, dt, *consts, st0, tail0)


def _projssd_body(u_ref, wz_ref, wx_ref, wdt_ref, *refs, q, n_sub, blocks_per_seq):
    k = _SsdConsts(*refs[:N_SSD_CONSTS])
    st0_ref, tail0_ref, y_ref, z_scr, xbc_scr, dt_scr, st_scr, xpad_scr, xbd_scr = refs[N_SSD_CONSTS:]
    i = pl.program_id(0)

    @pl.when((i == 0) | ((i + blocks_per_seq - 1) % blocks_per_seq == 0))
    def _():
        st_scr[...] = st0_ref[...]
        xpad_scr[0:TAIL_ROWS, :] = tail0_ref[...]

    @pl.when(i == 0)
    def _():
        for scr in (z_scr, xbc_scr, dt_scr):
            scr[1] = jnp.zeros(scr.shape[1:], scr.dtype)

    slot = i % 2
    _run(_ssd_stages(z_scr.at[1 - slot], xbc_scr.at[1 - slot], dt_scr.at[1 - slot], k, y_ref,
                     st_scr, xpad_scr, xbd_scr, q=q, n_sub=n_sub),
         _proj_stages(u_ref, (wz_ref, wx_ref, wdt_ref), (z_scr.at[slot], xbc_scr.at[slot], dt_scr.at[slot])))


def _projssd_call(u2, wz, wx, wdt, consts, st0, tail0, bsz, seq, q, n_sub):
    t = q * n_sub
    assert seq % t == 0 and t >= SSD_CONV - 1
    blocks_per_seq = seq // t
    n_blocks = bsz * blocks_per_seq
    cur = lambda i: (jnp.minimum(i, n_blocks - 1), 0)
    prev = lambda i: (jnp.maximum(i - 1, 0), 0)
    fixed = (wz, wx, wdt) + tuple(consts) + (st0, tail0)
    return pl.pallas_call(
        functools.partial(_projssd_body, q=q, n_sub=n_sub, blocks_per_seq=blocks_per_seq),
        grid=(n_blocks + 1,),
        in_specs=[pl.BlockSpec((t, D_MODEL), cur)] + [_const_spec(c.shape) for c in fixed],
        out_specs=pl.BlockSpec((t, SSD_INNER), prev),
        out_shape=jax.ShapeDtypeStruct((bsz * seq, SSD_INNER), BF16),
        scratch_shapes=[pltpu.VMEM((2, t, SSD_INNER), BF16), pltpu.VMEM((2, t, SSD_XBC), BF16),
                        pltpu.VMEM((2, t, LANES), F32)] + _ssd_scratch(q, t),
        compiler_params=_params(("arbitrary",)),
        name="proj_ssd",
    )(u2, *fixed)


def _hg_levels(q):
    out, m = [], q // 2
    while m >= 1:
        out.append(m)
        m //= 2
    return out


def _hg_sum_matrices(q):
    t = np.arange(q)
    l, tt = t[:, None], t[None, :]
    mats = [tt <= l, tt > l]
    for m in _hg_levels(q):
        ref = (l // (2 * m)) * (2 * m) + m - 1
        lower = (l % (2 * m)) >= m
        mats.append(np.where(lower, (tt > ref) & (tt <= l), (tt > l) & (tt <= ref)))
    return np.concatenate(mats, 0).astype(np.float32)


def _hg_level_masks(q):
    t = np.arange(q)
    l, s = t[:, None], t[None, :]
    masks = [(l > s) & (((l ^ s) >> int(np.log2(m))) == 1) for m in _hg_levels(q)] + [l == s]
    return np.stack([np.concatenate([mk, mk], 1) for mk in masks]).astype(np.float32)


def _block_diag_pair(x):
    z = jnp.zeros((x.shape[0], HG_K), x.dtype)
    return jnp.concatenate([jnp.concatenate([x[:, :HG_K], z], 1), jnp.concatenate([z, x[:, HG_K:]], 1)], 0)


def _hgrn2_stages(q_ref, f_ref, i_ref, g_ref, lbp_ref, nw_ref, sums_ref, masks_ref, y_ref, st_scr,
                  *, q, n_sub, layer):
    lbp = lbp_ref[...]
    e = jnp.exp(lbp - jnp.max(lbp, axis=0, keepdims=True))
    lb = jnp.sum(e[:layer + 1], axis=0, keepdims=True) / jnp.sum(e, axis=0, keepdims=True)

    n_lev = len(_hg_levels(q))

    def chunk(s):
        rows = slice(s * q, (s + 1) * q)
        f = lb + (1.0 - lb) * jax.nn.sigmoid(f_ref[rows, :].astype(F32))
        kk_b = (1.0 - f).astype(BF16)
        qs_b = _silu(q_ref[rows, :].astype(F32)).astype(BF16)
        v = i_ref[rows, :]
        lf = jnp.log(f) * np.float32(np.log2(np.e))
        p = jnp.exp2(_dot(sums_ref[...], jnp.concatenate(_split_terms(lf, EXPAND_TERMS), axis=0)))
        p_last = p[q - 1:q]
        p = p.astype(BF16)
        wq = [qs_b * p[(2 + j) * q:(3 + j) * q] for j in range(n_lev)] + [qs_b]
        wk = [kk_b * p[(2 + j) * q:(3 + j) * q] for j in range(n_lev)] + [kk_b]
        qg = qs_b * p[0:q]
        kd = kk_b * p[q:2 * q]
        yield

        for pair in range(HG_HEADS // 2):
            pl2 = slice(2 * pair * HG_K, 2 * (pair + 1) * HG_K)
            att = jnp.zeros((q, 2 * q), BF16)
            for j in range(n_lev + 1):
                a_j = _dot_nt(wq[j][:, pl2], _block_diag_pair(wk[j][:, pl2]))
                att = att + a_j.astype(BF16) * masks_ref[j]
            st = st_scr[pair]
            o = _dot(att, _block_diag_pair(v[:, pl2])) + _dot_nt(qg[:, pl2], st.astype(BF16))
            for hh in range(2):
                h = 2 * pair + hh
                hl = slice(h * HG_K, (h + 1) * HG_K)
                bl = slice(hh * HG_K, (hh + 1) * HG_K)
                st_scr[pair, bl, bl] = st[bl, bl] * p_last[:, hl] + _dot_tn(v[:, hl], kd[:, hl])
                o_h = o[:, bl]
                o_h = o_h * lax.rsqrt(jnp.mean(o_h * o_h, axis=-1, keepdims=True) + EPS) * nw_ref[:, hl]
                y_ref[rows, hl] = (o_h * _silu(g_ref[rows, hl].astype(F32))).astype(y_ref.dtype)
            yield

    for s in range(n_sub):
        yield from chunk(s)


def _hgrn2_body(q_ref, f_ref, i_ref, g_ref, lbp_ref, nw_ref, sums_ref, masks_ref, st0_ref,
                y_ref, stout_ref, st_scr, *, q, n_sub, layer):
    c = pl.program_id(1)

    @pl.when(c == 0)
    def _():
        st_scr[...] = st0_ref[...]

    _run(_hgrn2_stages(q_ref.at[0], f_ref.at[0], i_ref.at[0], g_ref.at[0], lbp_ref, nw_ref, sums_ref, masks_ref,
                       y_ref.at[0], st_scr, q=q, n_sub=n_sub, layer=layer))

    @pl.when(c == pl.num_programs(1) - 1)
    def _():
        stout_ref[0] = st_scr[...]


def _hgrn2_call(qq, ff, ii, gg, lbp, nw, st0, q, n_sub, layer):
    bsz, seq, _ = qq.shape
    t = q * n_sub
    assert seq % t == 0 and q & (q - 1) == 0
    sums = jnp.asarray(_stack(_hg_sum_matrices(q), EXPAND_TERMS, 1), BF16)
    masks = jnp.asarray(_hg_level_masks(q), BF16)
    blk = lambda b, c: (b, c, 0)
    per_b4 = lambda b, c: (b, 0, 0, 0)
    tok = pl.BlockSpec((1, t, HG_WIDTH), blk)
    st_shape = (HG_HEADS // 2, 2 * HG_K, 2 * HG_K)
    return pl.pallas_call(
        functools.partial(_hgrn2_body, q=q, n_sub=n_sub, layer=layer),
        grid=(bsz, seq // t),
        in_specs=[tok, tok, tok, tok, _const_spec(lbp.shape), _const_spec(nw.shape),
                  _const_spec(sums.shape), _const_spec(masks.shape), _const_spec(st0.shape)],
        out_specs=[tok, pl.BlockSpec((1,) + st_shape, per_b4)],
        out_shape=[jax.ShapeDtypeStruct((bsz, seq, HG_WIDTH), BF16),
                   jax.ShapeDtypeStruct((bsz,) + st_shape, F32)],
        scratch_shapes=[pltpu.VMEM(st_shape, F32)],
        compiler_params=_params(("arbitrary", "arbitrary")),
        name="hgrn2",
    )(qq, ff, ii, gg, lbp, nw, sums, masks, st0)


def _tail_body(h_ref, ya_ref, yb_ref, ga_ref, gb_ref, wa_ref, wb_ref, wo_ref,
               nw_ref, wgu_ref, wd_ref, fnw_ref, o_ref, acc_ref):
    merged = (jax.nn.sigmoid(ga_ref[...].astype(F32)) * _dot(ya_ref[...], wa_ref[...])
              + jax.nn.sigmoid(gb_ref[...].astype(F32)) * _dot(yb_ref[...], wb_ref[...]))
    h = h_ref[...] + _dot(merged.astype(BF16), wo_ref[...])
    xn = _rms(h, nw_ref[...]).astype(BF16)
    _run(_ffn_stages(xn, wgu_ref, wd_ref, acc_ref))
    h = h + acc_ref[...]
    o_ref[...] = _rms(h, fnw_ref[...])


def _tail_call(h, ya, yb, ga, gb, wa, wb, wo, nw, wgu, wd, fnw, tm):
    m = h.shape[0]
    row = lambda i: (i, 0)
    tile = pl.BlockSpec((tm, D_MODEL), row)
    return pl.pallas_call(
        _tail_body,
        grid=(m // tm,),
        in_specs=[tile, tile, tile, tile, tile,
                  _const_spec(wa.shape), _const_spec(wb.shape), _const_spec(wo.shape),
                  _const_spec((1, D_MODEL)),
                  _const_spec(wgu.shape), _const_spec(wd.shape),
                  _const_spec((1, D_MODEL))],
        out_specs=tile,
        out_shape=jax.ShapeDtypeStruct((m, D_MODEL), F32),
        scratch_shapes=[pltpu.VMEM((tm, D_MODEL), F32)],
        compiler_params=_params(("arbitrary",)),
        name="tail",
    )(h, ya, yb, ga, gb, wa, wb, wo, nw, wgu, wd, fnw)


def _ffn_weights(w_gu, w_down):
    return w_gu.astype(BF16), (0.5 * w_down).astype(BF16)


def _pad_lanes(v):
    return jnp.zeros((1, LANES), F32).at[0, :v.shape[0]].set(v.astype(F32))


class _Tiling(NamedTuple):
    rows: int
    chunk: int
    ssd_sub: int
    hg_sub: int


MAIN_TILING = _Tiling(rows=512, chunk=128, ssd_sub=4, hg_sub=4)


def kernel(x, meta_tokens, ffn1_norm, ffn1_w_gu, ffn1_w_down, mix_norm, w_in, ssd_conv_w, ssd_conv_b,
           ssd_dt_bias, ssd_a_log, ssd_d, ssd_norm, hg_lower_bound, hg_norm, w_branch_a, w_branch_b,
           w_out, ffn2_norm, ffn2_w_gu, ffn2_w_down, final_norm):
    bsz, seq, d = x.shape
    assert d == D_MODEL and ffn1_norm.shape[0] == 1, "single-layer block of width D_MODEL only"
    n_meta = meta_tokens.shape[0]
    row = lambda v: v.reshape(1, -1).astype(F32)
    r3 = lambda t, b, s: t.reshape(b, s, t.shape[-1])

    wgu1, wd1 = _ffn_weights(ffn1_w_gu[0], ffn1_w_down[0])
    wgu2, wd2 = _ffn_weights(ffn2_w_gu[0], ffn2_w_down[0])
    sizes = (SSD_INNER, SSD_XBC, SSD_HEADS, HG_WIDTH, HG_WIDTH, HG_WIDTH, HG_WIDTH, D_MODEL, D_MODEL)
    offs = np.cumsum((0,) + sizes)
    w_parts = [w_in[0][:, offs[j]:offs[j + 1]].astype(BF16) for j in range(len(sizes))]
    w_parts[2] = jnp.pad(w_parts[2], ((0, 0), (0, LANES - SSD_HEADS)))
    w_ssd, w_rest = w_parts[:3], w_parts[3:]
    ssd_dtypes, rest_dtypes = [BF16, BF16, F32], [BF16] * 6

    ssd_p = (ssd_conv_w[0].astype(F32), row(ssd_conv_b[0]), _pad_lanes(ssd_dt_bias[0]), _pad_lanes(ssd_a_log[0]),
             row(jnp.repeat(ssd_d[0], SSD_P)), row(ssd_norm[0]))
    hg_p = (hg_lower_bound.astype(F32), row(hg_norm[0]))

    _, u_m = _ffn1_call(meta_tokens.astype(F32), row(ffn1_norm[0]), wgu1, wd1, row(mix_norm[0]), n_meta)
    proj_m = _inproj_call(u_m, w_parts, ssd_dtypes + rest_dtypes, n_meta)
    zero_ssd = (jnp.zeros((SSD_G, SSD_N, SSD_GW), F32), jnp.zeros((TAIL_ROWS, SSD_XBC), F32))
    zero_hg = jnp.zeros((HG_HEADS // 2, 2 * HG_K, 2 * HG_K), F32)
    _, st_ssd, tail = _ssd_call(*(r3(t, 1, n_meta) for t in proj_m[:3]), _ssd_consts(*ssd_p, n_meta), *zero_ssd,
                                q=n_meta, n_sub=1)
    _, st_hg = _hgrn2_call(*(r3(t, 1, n_meta) for t in proj_m[3:7]), *hg_p, zero_hg, q=n_meta, n_sub=1, layer=0)

    tl = MAIN_TILING
    x2 = x.reshape(bsz * seq, d)
    h1, u = _ffn1_call(x2, row(ffn1_norm[0]), wgu1, wd1, row(mix_norm[0]), tl.rows)
    ya = _projssd_call(u, *w_ssd, _ssd_consts(*ssd_p, tl.chunk), st_ssd[0], tail[0], bsz, seq,
                       q=tl.chunk, n_sub=tl.ssd_sub)
    hq, hf, hi, hg, ga, gb = _inproj_call(u, w_rest, rest_dtypes, tl.rows)
    yb, _ = _hgrn2_call(*(r3(t, bsz, seq) for t in (hq, hf, hi, hg)), *hg_p, st_hg[0],
                        q=tl.chunk, n_sub=tl.hg_sub, layer=0)
    out = _tail_call(h1, ya, yb.reshape(bsz * seq, -1), ga, gb,
                     w_branch_a[0].astype(BF16), w_branch_b[0].astype(BF16), w_out[0].astype(BF16),
                     row(ffn2_norm[0]), wgu2, wd2, row(final_norm), tl.rows)
    return out.reshape(bsz, seq, d)
```

```python
import functools
from typing import NamedTuple

import numpy as np
import jax
import jax.numpy as jnp
from jax import lax
from jax.experimental import pallas as pl
from jax.experimental.pallas import tpu as pltpu

F32 = jnp.float32
BF16 = jnp.bfloat16
EPS = 1e-6

D_MODEL = 1024
D_FF = 2816
FF_CHUNK = 256
N_FF_CHUNKS = D_FF // FF_CHUNK
PROJ_CHUNK = 256

SSD_HEADS = 16
SSD_P = 64
SSD_G = 4
SSD_R = SSD_HEADS // SSD_G
SSD_N = 128
SSD_INNER = SSD_HEADS * SSD_P
SSD_CONV = 4
SSD_XBC = SSD_INNER + 2 * SSD_G * SSD_N
SSD_GW = SSD_R * SSD_P
LANES = 128
TAIL_ROWS = 8
CUMSUM_TERMS = 3
EXPAND_TERMS = 2

HG_HEADS = 8
HG_K = 128
HG_WIDTH = HG_HEADS * HG_K

VMEM_LIMIT_BYTES = 56 * 1024 * 1024


def _params(sem):
    return pltpu.CompilerParams(dimension_semantics=sem, vmem_limit_bytes=VMEM_LIMIT_BYTES)


def _const_spec(shape):
    zeros = (0,) * len(shape)
    return pl.BlockSpec(shape, lambda *_: zeros, pipeline_mode=pl.Buffered(1))


def _rms(x, w):
    return x * lax.rsqrt(jnp.mean(x * x, axis=-1, keepdims=True) + EPS) * w


def _silu(x):
    return x * jax.nn.sigmoid(x)


def _split_terms(x, n):
    terms, rest = [], x
    for i in range(n):
        terms.append(rest.astype(BF16))
        if i + 1 < n:
            rest = rest - terms[-1].astype(F32)
    return terms


def _dot(a, b):
    return jnp.dot(a, b, preferred_element_type=F32)


def _dot_nt(a, b):
    return lax.dot_general(a, b, (((1,), (1,)), ((), ())), preferred_element_type=F32)


def _dot_tn(a, b):
    return lax.dot_general(a, b, (((0,), (0,)), ((), ())), preferred_element_type=F32)


def _stack(mat, n, axis):
    return np.concatenate([mat] * n, axis=axis)


def _merged(*stage_generators):
    live = list(stage_generators)
    while live:
        for gen in list(live):
            if next(gen, StopIteration) is StopIteration:
                live.remove(gen)
        yield


def _run(*stage_generators):
    for _ in _merged(*stage_generators):
        pass


def _ffn_stages(xn, wgu_ref, wd_ref, acc_ref):
    for c in range(N_FF_CHUNKS):
        g = _dot(xn, wgu_ref[:, c * FF_CHUNK:(c + 1) * FF_CHUNK])
        u = _dot(xn, wgu_ref[:, D_FF + c * FF_CHUNK:D_FF + (c + 1) * FF_CHUNK])
        d = _dot((_silu(g) * u).astype(BF16), wd_ref[c * FF_CHUNK:(c + 1) * FF_CHUNK, :])
        if c == 0:
            acc_ref[...] = d
        else:
            acc_ref[...] += d
        yield


def _ffn1_body(x_ref, nw_ref, wgu_ref, wd_ref, nw2_ref, h_ref, u_ref, acc_ref):
    x = x_ref[...]
    xn = _rms(x, nw_ref[...]).astype(BF16)
    _run(_ffn_stages(xn, wgu_ref, wd_ref, acc_ref))
    h = x + acc_ref[...]
    h_ref[...] = h
    u_ref[...] = _rms(h, nw2_ref[...]).astype(BF16)


def _ffn1_call(x2, nw, wgu, wd, nw2, tm):
    m = x2.shape[0]
    row = lambda i: (i, 0)
    return pl.pallas_call(
        _ffn1_body,
        grid=(m // tm,),
        in_specs=[
            pl.BlockSpec((tm, D_MODEL), row),
            _const_spec((1, D_MODEL)),
            _const_spec(wgu.shape), _const_spec(wd.shape),
            _const_spec((1, D_MODEL)),
        ],
        out_specs=[pl.BlockSpec((tm, D_MODEL), row), pl.BlockSpec((tm, D_MODEL), row)],
        out_shape=[jax.ShapeDtypeStruct((m, D_MODEL), F32), jax.ShapeDtypeStruct((m, D_MODEL), BF16)],
        scratch_shapes=[pltpu.VMEM((tm, D_MODEL), F32)],
        compiler_params=_params(("arbitrary",)),
        name="ffn1",
    )(x2, nw, wgu, wd, nw2)


def _proj_stages(u_ref, w_refs, o_refs):
    u = u_ref[...]
    for w_ref, o_ref in zip(w_refs, o_refs):
        width = w_ref.shape[1]
        step = min(PROJ_CHUNK, width)
        for lo in range(0, width, step):
            o_ref[:, lo:lo + step] = _dot(u, w_ref[:, lo:lo + step]).astype(o_ref.dtype)
            yield


def _inproj_body(u_ref, *refs):
    n = len(refs) // 2
    _run(_proj_stages(u_ref, refs[:n], refs[n:]))


def _inproj_call(u2, weights, out_dtypes, tm):
    m = u2.shape[0]
    row = lambda i: (i, 0)
    return pl.pallas_call(
        _inproj_body,
        grid=(m // tm,),
        in_specs=[pl.BlockSpec((tm, D_MODEL), row)] + [_const_spec(w.shape) for w in weights],
        out_specs=[pl.BlockSpec((tm, w.shape[1]), row) for w in weights],
        out_shape=[jax.ShapeDtypeStruct((m, w.shape[1]), dt) for w, dt in zip(weights, out_dtypes)],
        compiler_params=_params(("arbitrary",)),
        name="inproj",
    )(u2, *weights)


class _SsdConsts(NamedTuple):
    cw: jax.Array
    cb: jax.Array
    dtb: jax.Array
    alog: jax.Array
    dsk: jax.Array
    nw: jax.Array
    lincl: jax.Array
    eye: jax.Array
    expand: jax.Array
    hmask: jax.Array


def _ssd_consts(cw, cb, dtb, alog, dsk, nw, q):
    expand = np.zeros((LANES, SSD_INNER), np.float32)
    hmask = np.zeros((SSD_R, q, SSD_INNER), np.float32)
    for h in range(SSD_HEADS):
        expand[h, h * SSD_P:(h + 1) * SSD_P] = 1.0
        hmask[h % SSD_R, :, h * SSD_P:(h + 1) * SSD_P] = 1.0
    return _SsdConsts(
        cw, cb, dtb, alog, dsk, nw,
        jnp.asarray(_stack(np.tril(np.ones((q, q), np.float32)), CUMSUM_TERMS, 1), BF16),
        jnp.asarray(_stack(np.eye(LANES, dtype=np.float32), CUMSUM_TERMS, 1), BF16),
        jnp.asarray(_stack(expand, EXPAND_TERMS, 0), BF16),
        jnp.asarray(hmask, BF16))


def _ssd_stages(z_ref, xbc_ref, dt_ref, k, y_ref, st_scr, xpad_scr, xbd_scr, *, q, n_sub):
    t = n_sub * q
    xpad_scr[TAIL_ROWS:TAIL_ROWS + t, :] = xbc_ref[...].astype(F32)
    neg_a = -jnp.exp(k.alog[...])
    li = lax.broadcasted_iota(jnp.int32, (q, q), 0)
    si = lax.broadcasted_iota(jnp.int32, (q, q), 1)
    causal = li >= si

    for s in range(n_sub):
        rows = slice(s * q, (s + 1) * q)
        conv = k.cb[...]
        for tap in reversed(range(SSD_CONV)):
            lo = TAIL_ROWS - (SSD_CONV - 1) + tap + s * q
            conv = conv + k.cw[tap:tap + 1, :] * xpad_scr[lo:lo + q, :]
        xc = _silu(conv)
        xs = xc[:, :SSD_INNER]
        bm = xc[:, SSD_INNER:SSD_INNER + SSD_G * SSD_N]
        cm = xc[:, SSD_INNER + SSD_G * SSD_N:]
        dt = jax.nn.softplus(dt_ref[rows, :] + k.dtb[...])
        a_cs = _dot(k.lincl[...], jnp.concatenate(_split_terms(dt * neg_a, CUMSUM_TERMS), axis=0))
        a_last = a_cs[q - 1:q, :]
        a_cs_t = _dot_nt(k.eye[...], jnp.concatenate(_split_terms(a_cs, CUMSUM_TERMS), axis=1))
        cols = jnp.concatenate([dt, jnp.exp(a_cs), dt * jnp.exp(a_last - a_cs)], axis=0)
        cols_x = _dot(jnp.concatenate(_split_terms(cols, EXPAND_TERMS), axis=1), k.expand[...])
        dt_x, ea_x, dtdec_x = cols_x[0:q], cols_x[q:2 * q], cols_x[2 * q:3 * q]
        x_dt = (xs * dt_x).astype(BF16)
        x_dec = (xs * dtdec_x).astype(BF16)
        x_heads = [x_dt * k.hmask[r] for r in range(SSD_R)]
        yield
        for g in range(SSD_G):
            gl = slice(g * SSD_GW, (g + 1) * SSD_GW)
            b_g = bm[:, g * SSD_N:(g + 1) * SSD_N].astype(BF16)
            c_g = cm[:, g * SSD_N:(g + 1) * SSD_N].astype(BF16)
            cbm = jnp.where(causal, _dot_nt(c_g, b_g), 0.0)
            m_cat = jnp.concatenate(
                [(cbm * jnp.exp(jnp.minimum(a_cs[:, h:h + 1] - a_cs_t[h:h + 1, :], 0.0))).astype(BF16)
                 for h in range(g * SSD_R, (g + 1) * SSD_R)], axis=1)
            for r in range(SSD_R):
                xbd_scr[r * q:(r + 1) * q, :] = x_heads[r][:, gl]
            st = st_scr[g]
            y = _dot(m_cat, xbd_scr[...]) + _dot(c_g, st.astype(BF16)) * ea_x[:, gl] + xs[:, gl] * k.dsk[:, gl]
            st_scr[g] = st * ea_x[q - 1:q, gl] + _dot_tn(b_g, x_dec[:, gl])
            yg = y * _silu(z_ref[rows, gl].astype(F32))
            yg = yg * lax.rsqrt(jnp.mean(yg * yg, axis=-1, keepdims=True) + EPS)
            y_ref[rows, gl] = (yg * k.nw[:, gl]).astype(y_ref.dtype)
            yield
    xpad_scr[TAIL_ROWS - (SSD_CONV - 1):TAIL_ROWS, :] = xpad_scr[TAIL_ROWS + t - (SSD_CONV - 1):TAIL_ROWS + t, :]
    yield


N_SSD_CONSTS = len(_SsdConsts._fields)


def _ssd_body(z_ref, xbc_ref, dt_ref, *refs, q, n_sub):
    k = _SsdConsts(*refs[:N_SSD_CONSTS])
    st0_ref, tail0_ref, y_ref, stout_ref, tailout_ref, st_scr, xpad_scr, xbd_scr = refs[N_SSD_CONSTS:]
    c = pl.program_id(1)

    @pl.when(c == 0)
    def _():
        st_scr[...] = st0_ref[...]
        xpad_scr[0:TAIL_ROWS, :] = tail0_ref[...]

    _run(_ssd_stages(z_ref.at[0], xbc_ref.at[0], dt_ref.at[0], k, y_ref.at[0], st_scr, xpad_scr, xbd_scr,
                     q=q, n_sub=n_sub))

    @pl.when(c == pl.num_programs(1) - 1)
    def _():
        stout_ref[0] = st_scr[...]
        tailout_ref[0] = xpad_scr[0:TAIL_ROWS, :]


def _ssd_scratch(q, t):
    return [pltpu.VMEM((SSD_G, SSD_N, SSD_GW), F32),
            pltpu.VMEM((TAIL_ROWS + t, SSD_XBC), F32),
            pltpu.VMEM((SSD_R * q, SSD_GW), BF16)]


def _ssd_call(z, xbc, dt, consts, st0, tail0, q, n_sub):
    bsz, seq, _ = z.shape
    t = q * n_sub
    assert seq % t == 0 and t >= SSD_CONV - 1
    blk = lambda b, c: (b, c, 0)
    per_b3 = lambda b, c: (b, 0, 0)
    per_b4 = lambda b, c: (b, 0, 0, 0)
    return pl.pallas_call(
        functools.partial(_ssd_body, q=q, n_sub=n_sub),
        grid=(bsz, seq // t),
        in_specs=[
            pl.BlockSpec((1, t, SSD_INNER), blk),
            pl.BlockSpec((1, t, SSD_XBC), blk),
            pl.BlockSpec((1, t, LANES), blk),
        ] + [_const_spec(c.shape) for c in consts] + [_const_spec(st0.shape), _const_spec(tail0.shape)],
        out_specs=[
            pl.BlockSpec((1, t, SSD_INNER), blk),
            pl.BlockSpec((1, SSD_G, SSD_N, SSD_GW), per_b4),
            pl.BlockSpec((1, TAIL_ROWS, SSD_XBC), per_b3),
        ],
        out_shape=[
            jax.ShapeDtypeStruct((bsz, seq, SSD_INNER), BF16),
            jax.ShapeDtypeStruct((bsz, SSD_G, SSD_N, SSD_GW), F32),
            jax.ShapeDtypeStruct((bsz, TAIL_ROWS, SSD_XBC), F32),
        ],
        scratch_shapes=_ssd_scratch(q, t),
        compiler_params=_params(("arbitrary", "arbitrary")),
        name="ssd",
    )(z, xbc, dt, *consts, st0, tail0)


def _projssd_body(u_ref, wz_ref, wx_ref, wdt_ref, *refs, q, n_sub, blocks_per_seq):
    k = _SsdConsts(*refs[:N_SSD_CONSTS])
    st0_ref, tail0_ref, y_ref, z_scr, xbc_scr, dt_scr, st_scr, xpad_scr, xbd_scr = refs[N_SSD_CONSTS:]
    i = pl.program_id(0)

    @pl.when((i == 0) | ((i + blocks_per_seq - 1) % blocks_per_seq == 0))
    def _():
        st_scr[...] = st0_ref[...]
        xpad_scr[0:TAIL_ROWS, :] = tail0_ref[...]

    @pl.when(i == 0)
    def _():
        for scr in (z_scr, xbc_scr, dt_scr):
            scr[1] = jnp.zeros(scr.shape[1:], scr.dtype)

    slot = i % 2
    _run(_ssd_stages(z_scr.at[1 - slot], xbc_scr.at[1 - slot], dt_scr.at[1 - slot], k, y_ref,
                     st_scr, xpad_scr, xbd_scr, q=q, n_sub=n_sub),
         _proj_stages(u_ref, (wz_ref, wx_ref, wdt_ref), (z_scr.at[slot], xbc_scr.at[slot], dt_scr.at[slot])))


def _projssd_call(u2, wz, wx, wdt, consts, st0, tail0, bsz, seq, q, n_sub):
    t = q * n_sub
    assert seq % t == 0 and t >= SSD_CONV - 1
    blocks_per_seq = seq // t
    n_blocks = bsz * blocks_per_seq
    cur = lambda i: (jnp.minimum(i, n_blocks - 1), 0)
    prev = lambda i: (jnp.maximum(i - 1, 0), 0)
    fixed = (wz, wx, wdt) + tuple(consts) + (st0, tail0)
    return pl.pallas_call(
        functools.partial(_projssd_body, q=q, n_sub=n_sub, blocks_per_seq=blocks_per_seq),
        grid=(n_blocks + 1,),
        in_specs=[pl.BlockSpec((t, D_MODEL), cur)] + [_const_spec(c.shape) for c in fixed],
        out_specs=pl.BlockSpec((t, SSD_INNER), prev),
        out_shape=jax.ShapeDtypeStruct((bsz * seq, SSD_INNER), BF16),
        scratch_shapes=[pltpu.VMEM((2, t, SSD_INNER), BF16), pltpu.VMEM((2, t, SSD_XBC), BF16),
                        pltpu.VMEM((2, t, LANES), F32)] + _ssd_scratch(q, t),
        compiler_params=_params(("arbitrary",)),
        name="proj_ssd",
    )(u2, *fixed)


def _hg_levels(q):
    out, m = [], q // 2
    while m >= 1:
        out.append(m)
        m //= 2
    return out


SUBLANES = 8


def _hg_matmul_levels(q):
    return [m for m in _hg_levels(q) if 2 * m < SUBLANES]


def _hg_sum_matrices(q):
    t = np.arange(q)
    l, tt = t[:, None], t[None, :]
    mats = [tt <= l]
    for m in _hg_matmul_levels(q):
        ref = (l // (2 * m)) * (2 * m) + m - 1
        lower = (l % (2 * m)) >= m
        mats.append(np.where(lower, (tt > ref) & (tt <= l), (tt > l) & (tt <= ref)))
    return np.concatenate(mats, 0).astype(np.float32)


def _hg_level_log_decays(gc, sums, q):
    small = _hg_matmul_levels(q)
    w = gc.shape[-1]
    out = []
    for m in _hg_levels(q):
        if m in small:
            j = 1 + small.index(m)
            out.append(sums[j * q:(j + 1) * q])
        elif m >= SUBLANES:
            g4 = gc.reshape(q // (2 * m), 2, m, w)
            upper, lower = g4[:, 0], g4[:, 1]
            ref = upper[:, m - 1:m, :]
            out.append(jnp.stack([ref - upper, lower - ref], axis=1).reshape(q, w))
        else:
            g3 = gc.reshape(q // (2 * m), 2 * m, w)
            d = g3 - g3[:, m - 1:m, :]
            out.append(jnp.minimum(d, -d).reshape(q, w))
    return out


def _hg_level_masks(q):
    t = np.arange(q)
    l, s = t[:, None], t[None, :]
    masks = [(l > s) & (((l ^ s) >> int(np.log2(m))) == 1) for m in _hg_levels(q)] + [l == s]
    return np.stack([np.concatenate([mk, mk], 1) for mk in masks]).astype(np.float32)


def _block_diag_pair(x):
    z = jnp.zeros((x.shape[0], HG_K), x.dtype)
    return jnp.concatenate([jnp.concatenate([x[:, :HG_K], z], 1), jnp.concatenate([z, x[:, HG_K:]], 1)], 0)


def _hgrn2_stages(q_ref, f_ref, i_ref, g_ref, lbp_ref, nw_ref, sums_ref, masks_ref, y_ref, st_scr,
                  *, q, n_sub, layer):
    lbp = lbp_ref[...]
    e = jnp.exp(lbp - jnp.max(lbp, axis=0, keepdims=True))
    lb = jnp.sum(e[:layer + 1], axis=0, keepdims=True) / jnp.sum(e, axis=0, keepdims=True)

    n_lev = len(_hg_levels(q))

    def chunk(s):
        rows = slice(s * q, (s + 1) * q)
        f = lb + (1.0 - lb) * jax.nn.sigmoid(f_ref[rows, :].astype(F32))
        kk_b = (1.0 - f).astype(BF16)
        qs_b = _silu(q_ref[rows, :].astype(F32)).astype(BF16)
        v = i_ref[rows, :]
        lf = jnp.log(f) * np.float32(np.log2(np.e))
        sums = _dot(sums_ref[...], jnp.concatenate(_split_terms(lf, EXPAND_TERMS), axis=0))
        gc = sums[0:q]
        p_inc = jnp.exp2(gc)
        p_last = p_inc[q - 1:q]
        p_lev = [jnp.exp2(e).astype(BF16) for e in _hg_level_log_decays(gc, sums, q)]
        wq = [qs_b * p for p in p_lev] + [qs_b]
        wk = [kk_b * p for p in p_lev] + [kk_b]
        qg = qs_b * p_inc.astype(BF16)
        kd = kk_b * jnp.exp2(gc[q - 1:q] - gc).astype(BF16)
        yield

        for pair in range(HG_HEADS // 2):
            pl2 = slice(2 * pair * HG_K, 2 * (pair + 1) * HG_K)
            att = jnp.zeros((q, 2 * q), BF16)
            for j in range(n_lev + 1):
                a_j = _dot_nt(wq[j][:, pl2], _block_diag_pair(wk[j][:, pl2]))
                att = att + a_j.astype(BF16) * masks_ref[j]
            st = st_scr[pair]
            o = _dot(att, _block_diag_pair(v[:, pl2])) + _dot_nt(qg[:, pl2], st.astype(BF16))
            for hh in range(2):
                h = 2 * pair + hh
                hl = slice(h * HG_K, (h + 1) * HG_K)
                bl = slice(hh * HG_K, (hh + 1) * HG_K)
                st_scr[pair, bl, bl] = st[bl, bl] * p_last[:, hl] + _dot_tn(v[:, hl], kd[:, hl])
                o_h = o[:, bl]
                o_h = o_h * lax.rsqrt(jnp.mean(o_h * o_h, axis=-1, keepdims=True) + EPS) * nw_ref[:, hl]
                y_ref[rows, hl] = (o_h * _silu(g_ref[rows, hl].astype(F32))).astype(y_ref.dtype)
            yield

    for s in range(n_sub):
        yield from chunk(s)


def _hgrn2_body(q_ref, f_ref, i_ref, g_ref, lbp_ref, nw_ref, sums_ref, masks_ref, st0_ref,
                y_ref, stout_ref, st_scr, *, q, n_sub, layer):
    c = pl.program_id(1)

    @pl.when(c == 0)
    def _():
        st_scr[...] = st0_ref[...]

    _run(_hgrn2_stages(q_ref.at[0], f_ref.at[0], i_ref.at[0], g_ref.at[0], lbp_ref, nw_ref, sums_ref, masks_ref,
                       y_ref.at[0], st_scr, q=q, n_sub=n_sub, layer=layer))

    @pl.when(c == pl.num_programs(1) - 1)
    def _():
        stout_ref[0] = st_scr[...]


def _hgrn2_call(qq, ff, ii, gg, lbp, nw, st0, q, n_sub, layer):
    bsz, seq, _ = qq.shape
    t = q * n_sub
    assert seq % t == 0 and q & (q - 1) == 0
    sums = jnp.asarray(_stack(_hg_sum_matrices(q), EXPAND_TERMS, 1), BF16)
    masks = jnp.asarray(_hg_level_masks(q), BF16)
    blk = lambda b, c: (b, c, 0)
    per_b4 = lambda b, c: (b, 0, 0, 0)
    tok = pl.BlockSpec((1, t, HG_WIDTH), blk)
    st_shape = (HG_HEADS // 2, 2 * HG_K, 2 * HG_K)
    return pl.pallas_call(
        functools.partial(_hgrn2_body, q=q, n_sub=n_sub, layer=layer),
        grid=(bsz, seq // t),
        in_specs=[tok, tok, tok, tok, _const_spec(lbp.shape), _const_spec(nw.shape),
                  _const_spec(sums.shape), _const_spec(masks.shape), _const_spec(st0.shape)],
        out_specs=[tok, pl.BlockSpec((1,) + st_shape, per_b4)],
        out_shape=[jax.ShapeDtypeStruct((bsz, seq, HG_WIDTH), BF16),
                   jax.ShapeDtypeStruct((bsz,) + st_shape, F32)],
        scratch_shapes=[pltpu.VMEM(st_shape, F32)],
        compiler_params=_params(("arbitrary", "arbitrary")),
        name="hgrn2",
    )(qq, ff, ii, gg, lbp, nw, sums, masks, st0)


def _tail_body(h_ref, ya_ref, yb_ref, ga_ref, gb_ref, wa_ref, wb_ref, wo_ref,
               nw_ref, wgu_ref, wd_ref, fnw_ref, o_ref, acc_ref):
    merged = (jax.nn.sigmoid(ga_ref[...].astype(F32)) * _dot(ya_ref[...], wa_ref[...])
              + jax.nn.sigmoid(gb_ref[...].astype(F32)) * _dot(yb_ref[...], wb_ref[...]))
    h = h_ref[...] + _dot(merged.astype(BF16), wo_ref[...])
    xn = _rms(h, nw_ref[...]).astype(BF16)
    _run(_ffn_stages(xn, wgu_ref, wd_ref, acc_ref))
    h = h + acc_ref[...]
    o_ref[...] = _rms(h, fnw_ref[...])


def _tail_call(h, ya, yb, ga, gb, wa, wb, wo, nw, wgu, wd, fnw, tm):
    m = h.shape[0]
    row = lambda i: (i, 0)
    tile = pl.BlockSpec((tm, D_MODEL), row)
    return pl.pallas_call(
        _tail_body,
        grid=(m // tm,),
        in_specs=[tile, tile, tile, tile, tile,
                  _const_spec(wa.shape), _const_spec(wb.shape), _const_spec(wo.shape),
                  _const_spec((1, D_MODEL)),
                  _const_spec(wgu.shape), _const_spec(wd.shape),
                  _const_spec((1, D_MODEL))],
        out_specs=tile,
        out_shape=jax.ShapeDtypeStruct((m, D_MODEL), F32),
        scratch_shapes=[pltpu.VMEM((tm, D_MODEL), F32)],
        compiler_params=_params(("arbitrary",)),
        name="tail",
    )(h, ya, yb, ga, gb, wa, wb, wo, nw, wgu, wd, fnw)


def _ffn_weights(w_gu, w_down):
    return w_gu.astype(BF16), (0.5 * w_down).astype(BF16)


def _pad_lanes(v):
    return jnp.zeros((1, LANES), F32).at[0, :v.shape[0]].set(v.astype(F32))


class _Tiling(NamedTuple):
    rows: int
    chunk: int
    ssd_sub: int
    hg_sub: int


MAIN_TILING = _Tiling(rows=512, chunk=128, ssd_sub=4, hg_sub=4)


def kernel(x, meta_tokens, ffn1_norm, ffn1_w_gu, ffn1_w_down, mix_norm, w_in, ssd_conv_w, ssd_conv_b,
           ssd_dt_bias, ssd_a_log, ssd_d, ssd_norm, hg_lower_bound, hg_norm, w_branch_a, w_branch_b,
           w_out, ffn2_norm, ffn2_w_gu, ffn2_w_down, final_norm):
    bsz, seq, d = x.shape
    assert d == D_MODEL and ffn1_norm.shape[0] == 1, "single-layer block of width D_MODEL only"
    n_meta = meta_tokens.shape[0]
    row = lambda v: v.reshape(1, -1).astype(F32)
    r3 = lambda t, b, s: t.reshape(b, s, t.shape[-1])

    wgu1, wd1 = _ffn_weights(ffn1_w_gu[0], ffn1_w_down[0])
    wgu2, wd2 = _ffn_weights(ffn2_w_gu[0], ffn2_w_down[0])
    sizes = (SSD_INNER, SSD_XBC, SSD_HEADS, HG_WIDTH, HG_WIDTH, HG_WIDTH, HG_WIDTH, D_MODEL, D_MODEL)
    offs = np.cumsum((0,) + sizes)
    w_parts = [w_in[0][:, offs[j]:offs[j + 1]].astype(BF16) for j in range(len(sizes))]
    w_parts[2] = jnp.pad(w_parts[2], ((0, 0), (0, LANES - SSD_HEADS)))
    w_ssd, w_rest = w_parts[:3], w_parts[3:]
    ssd_dtypes, rest_dtypes = [BF16, BF16, F32], [BF16] * 6

    ssd_p = (ssd_conv_w[0].astype(F32), row(ssd_conv_b[0]), _pad_lanes(ssd_dt_bias[0]), _pad_lanes(ssd_a_log[0]),
             row(jnp.repeat(ssd_d[0], SSD_P)), row(ssd_norm[0]))
    hg_p = (hg_lower_bound.astype(F32), row(hg_norm[0]))

    _, u_m = _ffn1_call(meta_tokens.astype(F32), row(ffn1_norm[0]), wgu1, wd1, row(mix_norm[0]), n_meta)
    proj_m = _inproj_call(u_m, w_parts, ssd_dtypes + rest_dtypes, n_meta)
    zero_ssd = (jnp.zeros((SSD_G, SSD_N, SSD_GW), F32), jnp.zeros((TAIL_ROWS, SSD_XBC), F32))
    zero_hg = jnp.zeros((HG_HEADS // 2, 2 * HG_K, 2 * HG_K), F32)
    _, st_ssd, tail = _ssd_call(*(r3(t, 1, n_meta) for t in proj_m[:3]), _ssd_consts(*ssd_p, n_meta), *zero_ssd,
                                q=n_meta, n_sub=1)
    _, st_hg = _hgrn2_call(*(r3(t, 1, n_meta) for t in proj_m[3:7]), *hg_p, zero_hg, q=n_meta, n_sub=1, layer=0)

    tl = MAIN_TILING
    x2 = x.reshape(bsz * seq, d)
    h1, u = _ffn1_call(x2, row(ffn1_norm[0]), wgu1, wd1, row(mix_norm[0]), tl.rows)
    ya = _projssd_call(u, *w_ssd, _ssd_consts(*ssd_p, tl.chunk), st_ssd[0], tail[0], bsz, seq,
                       q=tl.chunk, n_sub=tl.ssd_sub)
    hq, hf, hi, hg, ga, gb = _inproj_call(u, w_rest, rest_dtypes, tl.rows)
    yb, _ = _hgrn2_call(*(r3(t, bsz, seq) for t in (hq, hf, hi, hg)), *hg_p, st_hg[0],
                        q=tl.chunk, n_sub=tl.hg_sub, layer=0)
    out = _tail_call(h1, ya, yb.reshape(bsz * seq, -1), ga, gb,
                     w_branch_a[0].astype(BF16), w_branch_b[0].astype(BF16), w_out[0].astype(BF16),
                     row(ffn2_norm[0]), wgu2, wd2, row(final_norm), tl.rows)
    return out.reshape(bsz, seq, d)
```

```python
import functools
from typing import NamedTuple

import numpy as np
import jax
import jax.numpy as jnp
from jax import lax
from jax.experimental import pallas as pl
from jax.experimental.pallas import tpu as pltpu

F32 = jnp.float32
BF16 = jnp.bfloat16
EPS = 1e-6

D_MODEL = 1024
D_FF = 2816
FF_CHUNK = 256
N_FF_CHUNKS = D_FF // FF_CHUNK
PROJ_CHUNK = 256

SSD_HEADS = 16
SSD_P = 64
SSD_G = 4
SSD_R = SSD_HEADS // SSD_G
SSD_N = 128
SSD_INNER = SSD_HEADS * SSD_P
SSD_CONV = 4
SSD_XBC = SSD_INNER + 2 * SSD_G * SSD_N
SSD_GW = SSD_R * SSD_P
LANES = 128
TAIL_ROWS = 8
CUMSUM_TERMS = 3
EXPAND_TERMS = 2

HG_HEADS = 8
HG_K = 128
HG_WIDTH = HG_HEADS * HG_K

VMEM_LIMIT_BYTES = 56 * 1024 * 1024


def _params(sem):
    return pltpu.CompilerParams(dimension_semantics=sem, vmem_limit_bytes=VMEM_LIMIT_BYTES)


def _const_spec(shape):
    zeros = (0,) * len(shape)
    return pl.BlockSpec(shape, lambda *_: zeros, pipeline_mode=pl.Buffered(1))


def _rms(x, w):
    return x * lax.rsqrt(jnp.mean(x * x, axis=-1, keepdims=True) + EPS) * w


def _sigmoid(x):
    return 0.5 * jnp.tanh(0.5 * x) + 0.5


def _silu(x):
    return x * _sigmoid(x)


def _split_terms(x, n):
    terms, rest = [], x
    for i in range(n):
        terms.append(rest.astype(BF16))
        if i + 1 < n:
            rest = rest - terms[-1].astype(F32)
    return terms


def _dot(a, b):
    return jnp.dot(a, b, preferred_element_type=F32)


def _dot_nt(a, b):
    return lax.dot_general(a, b, (((1,), (1,)), ((), ())), preferred_element_type=F32)


def _dot_tn(a, b):
    return lax.dot_general(a, b, (((0,), (0,)), ((), ())), preferred_element_type=F32)


def _stack(mat, n, axis):
    return np.concatenate([mat] * n, axis=axis)


def _merged(*stage_generators):
    live = list(stage_generators)
    while live:
        for gen in list(live):
            if next(gen, StopIteration) is StopIteration:
                live.remove(gen)
        yield


def _run(*stage_generators):
    for _ in _merged(*stage_generators):
        pass


def _ffn_stages(xn, wgu_ref, wd_ref, acc_ref):
    for c in range(N_FF_CHUNKS):
        g = _dot(xn, wgu_ref[:, c * FF_CHUNK:(c + 1) * FF_CHUNK])
        u = _dot(xn, wgu_ref[:, D_FF + c * FF_CHUNK:D_FF + (c + 1) * FF_CHUNK])
        d = _dot((_silu(g) * u).astype(BF16), wd_ref[c * FF_CHUNK:(c + 1) * FF_CHUNK, :])
        if c == 0:
            acc_ref[...] = d
        else:
            acc_ref[...] += d
        yield


def _ffn1_body(x_ref, nw_ref, wgu_ref, wd_ref, nw2_ref, h_ref, u_ref, acc_ref):
    x = x_ref[...]
    xn = _rms(x, nw_ref[...]).astype(BF16)
    _run(_ffn_stages(xn, wgu_ref, wd_ref, acc_ref))
    h = x + acc_ref[...]
    h_ref[...] = h
    u_ref[...] = _rms(h, nw2_ref[...]).astype(BF16)


def _ffn1_call(x2, nw, wgu, wd, nw2, tm):
    m = x2.shape[0]
    row = lambda i: (i, 0)
    return pl.pallas_call(
        _ffn1_body,
        grid=(m // tm,),
        in_specs=[
            pl.BlockSpec((tm, D_MODEL), row),
            _const_spec((1, D_MODEL)),
            _const_spec(wgu.shape), _const_spec(wd.shape),
            _const_spec((1, D_MODEL)),
        ],
        out_specs=[pl.BlockSpec((tm, D_MODEL), row), pl.BlockSpec((tm, D_MODEL), row)],
        out_shape=[jax.ShapeDtypeStruct((m, D_MODEL), F32), jax.ShapeDtypeStruct((m, D_MODEL), BF16)],
        scratch_shapes=[pltpu.VMEM((tm, D_MODEL), F32)],
        compiler_params=_params(("arbitrary",)),
        name="ffn1",
    )(x2, nw, wgu, wd, nw2)


def _proj_stages(u_ref, w_refs, o_refs):
    u = u_ref[...]
    for w_ref, o_ref in zip(w_refs, o_refs):
        width = w_ref.shape[1]
        step = min(PROJ_CHUNK, width)
        for lo in range(0, width, step):
            o_ref[:, lo:lo + step] = _dot(u, w_ref[:, lo:lo + step]).astype(o_ref.dtype)
            yield


def _inproj_body(u_ref, *refs):
    n = len(refs) // 2
    _run(_proj_stages(u_ref, refs[:n], refs[n:]))


def _inproj_call(u2, weights, out_dtypes, tm):
    m = u2.shape[0]
    row = lambda i: (i, 0)
    return pl.pallas_call(
        _inproj_body,
        grid=(m // tm,),
        in_specs=[pl.BlockSpec((tm, D_MODEL), row)] + [_const_spec(w.shape) for w in weights],
        out_specs=[pl.BlockSpec((tm, w.shape[1]), row) for w in weights],
        out_shape=[jax.ShapeDtypeStruct((m, w.shape[1]), dt) for w, dt in zip(weights, out_dtypes)],
        compiler_params=_params(("arbitrary",)),
        name="inproj",
    )(u2, *weights)


class _SsdConsts(NamedTuple):
    cw: jax.Array
    cb: jax.Array
    dtb: jax.Array
    alog: jax.Array
    dsk: jax.Array
    nw: jax.Array
    lincl: jax.Array
    eye: jax.Array
    expand: jax.Array
    hmask: jax.Array


def _ssd_consts(cw, cb, dtb, alog, dsk, nw, q):
    expand = np.zeros((LANES, SSD_INNER), np.float32)
    hmask = np.zeros((SSD_R, q, SSD_INNER), np.float32)
    for h in range(SSD_HEADS):
        expand[h, h * SSD_P:(h + 1) * SSD_P] = 1.0
        hmask[h % SSD_R, :, h * SSD_P:(h + 1) * SSD_P] = 1.0
    return _SsdConsts(
        cw, cb, dtb, alog, dsk, nw,
        jnp.asarray(_stack(np.tril(np.ones((q, q), np.float32)), CUMSUM_TERMS, 1), BF16),
        jnp.asarray(_stack(np.eye(LANES, dtype=np.float32), CUMSUM_TERMS, 1), BF16),
        jnp.asarray(_stack(expand, EXPAND_TERMS, 0), BF16),
        jnp.asarray(hmask, BF16))


def _ssd_stages(z_ref, xbc_ref, dt_ref, k, y_ref, st_scr, xpad_scr, xbd_scr, *, q, n_sub):
    t = n_sub * q
    xpad_scr[TAIL_ROWS:TAIL_ROWS + t, :] = xbc_ref[...].astype(F32)
    neg_a = -jnp.exp(k.alog[...])
    li = lax.broadcasted_iota(jnp.int32, (q, q), 0)
    si = lax.broadcasted_iota(jnp.int32, (q, q), 1)
    causal = li >= si

    for s in range(n_sub):
        rows = slice(s * q, (s + 1) * q)
        conv = k.cb[...]
        for tap in reversed(range(SSD_CONV)):
            lo = TAIL_ROWS - (SSD_CONV - 1) + tap + s * q
            conv = conv + k.cw[tap:tap + 1, :] * xpad_scr[lo:lo + q, :]
        xc = _silu(conv)
        xs = xc[:, :SSD_INNER]
        bm = xc[:, SSD_INNER:SSD_INNER + SSD_G * SSD_N]
        cm = xc[:, SSD_INNER + SSD_G * SSD_N:]
        dt = jax.nn.softplus(dt_ref[rows, :] + k.dtb[...])
        a_cs = _dot(k.lincl[...], jnp.concatenate(_split_terms(dt * neg_a, CUMSUM_TERMS), axis=0))
        a_last = a_cs[q - 1:q, :]
        a_cs_t = _dot_nt(k.eye[...], jnp.concatenate(_split_terms(a_cs, CUMSUM_TERMS), axis=1))
        cols = jnp.concatenate([dt, jnp.exp(a_cs), dt * jnp.exp(a_last - a_cs)], axis=0)
        cols_x = _dot(jnp.concatenate(_split_terms(cols, EXPAND_TERMS), axis=1), k.expand[...])
        dt_x, ea_x, dtdec_x = cols_x[0:q], cols_x[q:2 * q], cols_x[2 * q:3 * q]
        x_dt = (xs * dt_x).astype(BF16)
        x_dec = (xs * dtdec_x).astype(BF16)
        x_heads = [x_dt * k.hmask[r] for r in range(SSD_R)]
        yield
        for g in range(SSD_G):
            gl = slice(g * SSD_GW, (g + 1) * SSD_GW)
            b_g = bm[:, g * SSD_N:(g + 1) * SSD_N].astype(BF16)
            c_g = cm[:, g * SSD_N:(g + 1) * SSD_N].astype(BF16)
            cbm = jnp.where(causal, _dot_nt(c_g, b_g), 0.0)
            m_cat = jnp.concatenate(
                [(cbm * jnp.exp(jnp.minimum(a_cs[:, h:h + 1] - a_cs_t[h:h + 1, :], 0.0))).astype(BF16)
                 for h in range(g * SSD_R, (g + 1) * SSD_R)], axis=1)
            for r in range(SSD_R):
                xbd_scr[r * q:(r + 1) * q, :] = x_heads[r][:, gl]
            st = st_scr[g]
            y = _dot(m_cat, xbd_scr[...]) + _dot(c_g, st.astype(BF16)) * ea_x[:, gl] + xs[:, gl] * k.dsk[:, gl]
            st_scr[g] = st * ea_x[q - 1:q, gl] + _dot_tn(b_g, x_dec[:, gl])
            yg = y * _silu(z_ref[rows, gl].astype(F32))
            yg = yg * lax.rsqrt(jnp.mean(yg * yg, axis=-1, keepdims=True) + EPS)
            y_ref[rows, gl] = (yg * k.nw[:, gl]).astype(y_ref.dtype)
            yield
    xpad_scr[TAIL_ROWS - (SSD_CONV - 1):TAIL_ROWS, :] = xpad_scr[TAIL_ROWS + t - (SSD_CONV - 1):TAIL_ROWS + t, :]
    yield


N_SSD_CONSTS = len(_SsdConsts._fields)


def _ssd_body(z_ref, xbc_ref, dt_ref, *refs, q, n_sub):
    k = _SsdConsts(*refs[:N_SSD_CONSTS])
    st0_ref, tail0_ref, y_ref, stout_ref, tailout_ref, st_scr, xpad_scr, xbd_scr = refs[N_SSD_CONSTS:]
    c = pl.program_id(1)

    @pl.when(c == 0)
    def _():
        st_scr[...] = st0_ref[...]
        xpad_scr[0:TAIL_ROWS, :] = tail0_ref[...]

    _run(_ssd_stages(z_ref.at[0], xbc_ref.at[0], dt_ref.at[0], k, y_ref.at[0], st_scr, xpad_scr, xbd_scr,
                     q=q, n_sub=n_sub))

    @pl.when(c == pl.num_programs(1) - 1)
    def _():
        stout_ref[0] = st_scr[...]
        tailout_ref[0] = xpad_scr[0:TAIL_ROWS, :]


def _ssd_scratch(q, t):
    return [pltpu.VMEM((SSD_G, SSD_N, SSD_GW), F32),
            pltpu.VMEM((TAIL_ROWS + t, SSD_XBC), F32),
            pltpu.VMEM((SSD_R * q, SSD_GW), BF16)]


def _ssd_call(z, xbc, dt, consts, st0, tail0, q, n_sub):
    bsz, seq, _ = z.shape
    t = q * n_sub
    assert seq % t == 0 and t >= SSD_CONV - 1
    blk = lambda b, c: (b, c, 0)
    per_b3 = lambda b, c: (b, 0, 0)
    per_b4 = lambda b, c: (b, 0, 0, 0)
    return pl.pallas_call(
        functools.partial(_ssd_body, q=q, n_sub=n_sub),
        grid=(bsz, seq // t),
        in_specs=[
            pl.BlockSpec((1, t, SSD_INNER), blk),
            pl.BlockSpec((1, t, SSD_XBC), blk),
            pl.BlockSpec((1, t, LANES), blk),
        ] + [_const_spec(c.shape) for c in consts] + [_const_spec(st0.shape), _const_spec(tail0.shape)],
        out_specs=[
            pl.BlockSpec((1, t, SSD_INNER), blk),
            pl.BlockSpec((1, SSD_G, SSD_N, SSD_GW), per_b4),
            pl.BlockSpec((1, TAIL_ROWS, SSD_XBC), per_b3),
        ],
        out_shape=[
            jax.ShapeDtypeStruct((bsz, seq, SSD_INNER), BF16),
            jax.ShapeDtypeStruct((bsz, SSD_G, SSD_N, SSD_GW), F32),
            jax.ShapeDtypeStruct((bsz, TAIL_ROWS, SSD_XBC), F32),
        ],
        scratch_shapes=_ssd_scratch(q, t),
        compiler_params=_params(("arbitrary", "arbitrary")),
        name="ssd",
    )(z, xbc, dt, *consts, st0, tail0)


def _projssd_body(u_ref, wz_ref, wx_ref, wdt_ref, *refs, q, n_sub, blocks_per_seq):
    k = _SsdConsts(*refs[:N_SSD_CONSTS])
    st0_ref, tail0_ref, y_ref, z_scr, xbc_scr, dt_scr, st_scr, xpad_scr, xbd_scr = refs[N_SSD_CONSTS:]
    i = pl.program_id(0)

    @pl.when((i == 0) | ((i + blocks_per_seq - 1) % blocks_per_seq == 0))
    def _():
        st_scr[...] = st0_ref[...]
        xpad_scr[0:TAIL_ROWS, :] = tail0_ref[...]

    @pl.when(i == 0)
    def _():
        for scr in (z_scr, xbc_scr, dt_scr):
            scr[1] = jnp.zeros(scr.shape[1:], scr.dtype)

    slot = i % 2
    _run(_ssd_stages(z_scr.at[1 - slot], xbc_scr.at[1 - slot], dt_scr.at[1 - slot], k, y_ref,
                     st_scr, xpad_scr, xbd_scr, q=q, n_sub=n_sub),
         _proj_stages(u_ref, (wz_ref, wx_ref, wdt_ref), (z_scr.at[slot], xbc_scr.at[slot], dt_scr.at[slot])))


def _projssd_call(u2, wz, wx, wdt, consts, st0, tail0, bsz, seq, q, n_sub):
    t = q * n_sub
    assert seq % t == 0 and t >= SSD_CONV - 1
    blocks_per_seq = seq // t
    n_blocks = bsz * blocks_per_seq
    cur = lambda i: (jnp.minimum(i, n_blocks - 1), 0)
    prev = lambda i: (jnp.maximum(i - 1, 0), 0)
    fixed = (wz, wx, wdt) + tuple(consts) + (st0, tail0)
    return pl.pallas_call(
        functools.partial(_projssd_body, q=q, n_sub=n_sub, blocks_per_seq=blocks_per_seq),
        grid=(n_blocks + 1,),
        in_specs=[pl.BlockSpec((t, D_MODEL), cur)] + [_const_spec(c.shape) for c in fixed],
        out_specs=pl.BlockSpec((t, SSD_INNER), prev),
        out_shape=jax.ShapeDtypeStruct((bsz * seq, SSD_INNER), BF16),
        scratch_shapes=[pltpu.VMEM((2, t, SSD_INNER), BF16), pltpu.VMEM((2, t, SSD_XBC), BF16),
                        pltpu.VMEM((2, t, LANES), F32)] + _ssd_scratch(q, t),
        compiler_params=_params(("arbitrary",)),
        name="proj_ssd",
    )(u2, *fixed)


def _hg_levels(q):
    out, m = [], q // 2
    while m >= 1:
        out.append(m)
        m //= 2
    return out


SUBLANES = 8


def _hg_matmul_levels(q):
    return [m for m in _hg_levels(q) if 2 * m < SUBLANES]


def _hg_sum_matrices(q):
    t = np.arange(q)
    l, tt = t[:, None], t[None, :]
    mats = [tt <= l]
    for m in _hg_matmul_levels(q):
        ref = (l // (2 * m)) * (2 * m) + m - 1
        lower = (l % (2 * m)) >= m
        mats.append(np.where(lower, (tt > ref) & (tt <= l), (tt > l) & (tt <= ref)))
    return np.concatenate(mats, 0).astype(np.float32)


def _hg_level_log_decays(gc, sums, q):
    small = _hg_matmul_levels(q)
    w = gc.shape[-1]
    out = []
    for m in _hg_levels(q):
        if m in small:
            j = 1 + small.index(m)
            out.append(sums[j * q:(j + 1) * q])
        elif m >= SUBLANES:
            g4 = gc.reshape(q // (2 * m), 2, m, w)
            upper, lower = g4[:, 0], g4[:, 1]
            ref = upper[:, m - 1:m, :]
            out.append(jnp.stack([ref - upper, lower - ref], axis=1).reshape(q, w))
        else:
            g3 = gc.reshape(q // (2 * m), 2 * m, w)
            d = g3 - g3[:, m - 1:m, :]
            out.append(jnp.minimum(d, -d).reshape(q, w))
    return out


def _hg_level_masks(q):
    t = np.arange(q)
    l, s = t[:, None], t[None, :]
    masks = [(l > s) & (((l ^ s) >> int(np.log2(m))) == 1) for m in _hg_levels(q)] + [l == s]
    return np.stack([np.concatenate([mk, mk], 1) for mk in masks]).astype(np.float32)


def _block_diag_pair(x):
    z = jnp.zeros((x.shape[0], HG_K), x.dtype)
    return jnp.concatenate([jnp.concatenate([x[:, :HG_K], z], 1), jnp.concatenate([z, x[:, HG_K:]], 1)], 0)


def _hgrn2_stages(q_ref, f_ref, i_ref, g_ref, lbp_ref, nw_ref, sums_ref, masks_ref, y_ref, st_scr,
                  *, q, n_sub, layer):
    lbp = lbp_ref[...]
    e = jnp.exp(lbp - jnp.max(lbp, axis=0, keepdims=True))
    lb = jnp.sum(e[:layer + 1], axis=0, keepdims=True) / jnp.sum(e, axis=0, keepdims=True)

    n_lev = len(_hg_levels(q))

    def chunk(s):
        rows = slice(s * q, (s + 1) * q)
        f = lb + (1.0 - lb) * _sigmoid(f_ref[rows, :].astype(F32))
        kk_b = (1.0 - f).astype(BF16)
        qs_b = _silu(q_ref[rows, :].astype(F32)).astype(BF16)
        v = i_ref[rows, :]
        lf = jnp.log(f) * np.float32(np.log2(np.e))
        sums = _dot(sums_ref[...], jnp.concatenate(_split_terms(lf, EXPAND_TERMS), axis=0))
        gc = sums[0:q]
        p_inc = jnp.exp2(gc)
        p_last = p_inc[q - 1:q]
        p_lev = [jnp.exp2(e).astype(BF16) for e in _hg_level_log_decays(gc, sums, q)]
        wq = [qs_b * p for p in p_lev] + [qs_b]
        wk = [kk_b * p for p in p_lev] + [kk_b]
        qg = qs_b * p_inc.astype(BF16)
        kd = kk_b * jnp.exp2(gc[q - 1:q] - gc).astype(BF16)
        yield

        for pair in range(HG_HEADS // 2):
            pl2 = slice(2 * pair * HG_K, 2 * (pair + 1) * HG_K)
            att = jnp.zeros((q, 2 * q), BF16)
            for j in range(n_lev + 1):
                a_j = _dot_nt(wq[j][:, pl2], _block_diag_pair(wk[j][:, pl2]))
                att = att + a_j.astype(BF16) * masks_ref[j]
            st = st_scr[pair]
            o = _dot(att, _block_diag_pair(v[:, pl2])) + _dot_nt(qg[:, pl2], st.astype(BF16))
            for hh in range(2):
                h = 2 * pair + hh
                hl = slice(h * HG_K, (h + 1) * HG_K)
                bl = slice(hh * HG_K, (hh + 1) * HG_K)
                st_scr[pair, bl, bl] = st[bl, bl] * p_last[:, hl] + _dot_tn(v[:, hl], kd[:, hl])
                o_h = o[:, bl]
                o_h = o_h * lax.rsqrt(jnp.mean(o_h * o_h, axis=-1, keepdims=True) + EPS) * nw_ref[:, hl]
                y_ref[rows, hl] = (o_h * _silu(g_ref[rows, hl].astype(F32))).astype(y_ref.dtype)
            yield

    for s in range(n_sub):
        yield from chunk(s)


def _hgrn2_body(q_ref, f_ref, i_ref, g_ref, lbp_ref, nw_ref, sums_ref, masks_ref, st0_ref,
                y_ref, stout_ref, st_scr, *, q, n_sub, layer):
    c = pl.program_id(1)

    @pl.when(c == 0)
    def _():
        st_scr[...] = st0_ref[...]

    _run(_hgrn2_stages(q_ref.at[0], f_ref.at[0], i_ref.at[0], g_ref.at[0], lbp_ref, nw_ref, sums_ref, masks_ref,
                       y_ref.at[0], st_scr, q=q, n_sub=n_sub, layer=layer))

    @pl.when(c == pl.num_programs(1) - 1)
    def _():
        stout_ref[0] = st_scr[...]


def _hgrn2_call(qq, ff, ii, gg, lbp, nw, st0, q, n_sub, layer):
    bsz, seq, _ = qq.shape
    t = q * n_sub
    assert seq % t == 0 and q & (q - 1) == 0
    sums = jnp.asarray(_stack(_hg_sum_matrices(q), EXPAND_TERMS, 1), BF16)
    masks = jnp.asarray(_hg_level_masks(q), BF16)
    blk = lambda b, c: (b, c, 0)
    per_b4 = lambda b, c: (b, 0, 0, 0)
    tok = pl.BlockSpec((1, t, HG_WIDTH), blk)
    st_shape = (HG_HEADS // 2, 2 * HG_K, 2 * HG_K)
    return pl.pallas_call(
        functools.partial(_hgrn2_body, q=q, n_sub=n_sub, layer=layer),
        grid=(bsz, seq // t),
        in_specs=[tok, tok, tok, tok, _const_spec(lbp.shape), _const_spec(nw.shape),
                  _const_spec(sums.shape), _const_spec(masks.shape), _const_spec(st0.shape)],
        out_specs=[tok, pl.BlockSpec((1,) + st_shape, per_b4)],
        out_shape=[jax.ShapeDtypeStruct((bsz, seq, HG_WIDTH), BF16),
                   jax.ShapeDtypeStruct((bsz,) + st_shape, F32)],
        scratch_shapes=[pltpu.VMEM(st_shape, F32)],
        compiler_params=_params(("arbitrary", "arbitrary")),
        name="hgrn2",
    )(qq, ff, ii, gg, lbp, nw, sums, masks, st0)


def _tail_body(h_ref, ya_ref, yb_ref, ga_ref, gb_ref, wa_ref, wb_ref, wo_ref,
               nw_ref, wgu_ref, wd_ref, fnw_ref, o_ref, acc_ref):
    merged = (_sigmoid(ga_ref[...].astype(F32)) * _dot(ya_ref[...], wa_ref[...])
              + _sigmoid(gb_ref[...].astype(F32)) * _dot(yb_ref[...], wb_ref[...]))
    h = h_ref[...] + _dot(merged.astype(BF16), wo_ref[...])
    xn = _rms(h, nw_ref[...]).astype(BF16)
    _run(_ffn_stages(xn, wgu_ref, wd_ref, acc_ref))
    h = h + acc_ref[...]
    o_ref[...] = _rms(h, fnw_ref[...])


def _tail_call(h, ya, yb, ga, gb, wa, wb, wo, nw, wgu, wd, fnw, tm):
    m = h.shape[0]
    row = lambda i: (i, 0)
    tile = pl.BlockSpec((tm, D_MODEL), row)
    return pl.pallas_call(
        _tail_body,
        grid=(m // tm,),
        in_specs=[tile, tile, tile, tile, tile,
                  _const_spec(wa.shape), _const_spec(wb.shape), _const_spec(wo.shape),
                  _const_spec((1, D_MODEL)),
                  _const_spec(wgu.shape), _const_spec(wd.shape),
                  _const_spec((1, D_MODEL))],
        out_specs=tile,
        out_shape=jax.ShapeDtypeStruct((m, D_MODEL), F32),
        scratch_shapes=[pltpu.VMEM((tm, D_MODEL), F32)],
        compiler_params=_params(("arbitrary",)),
        name="tail",
    )(h, ya, yb, ga, gb, wa, wb, wo, nw, wgu, wd, fnw)


def _ffn_weights(w_gu, w_down):
    return w_gu.astype(BF16), (0.5 * w_down).astype(BF16)


def _pad_lanes(v):
    return jnp.zeros((1, LANES), F32).at[0, :v.shape[0]].set(v.astype(F32))


class _Tiling(NamedTuple):
    rows: int
    chunk: int
    ssd_sub: int
    hg_sub: int


MAIN_TILING = _Tiling(rows=512, chunk=128, ssd_sub=4, hg_sub=8)


def kernel(x, meta_tokens, ffn1_norm, ffn1_w_gu, ffn1_w_down, mix_norm, w_in, ssd_conv_w, ssd_conv_b,
           ssd_dt_bias, ssd_a_log, ssd_d, ssd_norm, hg_lower_bound, hg_norm, w_branch_a, w_branch_b,
           w_out, ffn2_norm, ffn2_w_gu, ffn2_w_down, final_norm):
    bsz, seq, d = x.shape
    assert d == D_MODEL and ffn1_norm.shape[0] == 1, "single-layer block of width D_MODEL only"
    n_meta = meta_tokens.shape[0]
    row = lambda v: v.reshape(1, -1).astype(F32)
    r3 = lambda t, b, s: t.reshape(b, s, t.shape[-1])

    wgu1, wd1 = _ffn_weights(ffn1_w_gu[0], ffn1_w_down[0])
    wgu2, wd2 = _ffn_weights(ffn2_w_gu[0], ffn2_w_down[0])
    sizes = (SSD_INNER, SSD_XBC, SSD_HEADS, HG_WIDTH, HG_WIDTH, HG_WIDTH, HG_WIDTH, D_MODEL, D_MODEL)
    offs = np.cumsum((0,) + sizes)
    w_parts = [w_in[0][:, offs[j]:offs[j + 1]].astype(BF16) for j in range(len(sizes))]
    w_parts[2] = jnp.pad(w_parts[2], ((0, 0), (0, LANES - SSD_HEADS)))
    w_ssd, w_rest = w_parts[:3], w_parts[3:]
    ssd_dtypes, rest_dtypes = [BF16, BF16, F32], [BF16] * 6

    ssd_p = (ssd_conv_w[0].astype(F32), row(ssd_conv_b[0]), _pad_lanes(ssd_dt_bias[0]), _pad_lanes(ssd_a_log[0]),
             row(jnp.repeat(ssd_d[0], SSD_P)), row(ssd_norm[0]))
    hg_p = (hg_lower_bound.astype(F32), row(hg_norm[0]))

    _, u_m = _ffn1_call(meta_tokens.astype(F32), row(ffn1_norm[0]), wgu1, wd1, row(mix_norm[0]), n_meta)
    proj_m = _inproj_call(u_m, w_parts, ssd_dtypes + rest_dtypes, n_meta)
    zero_ssd = (jnp.zeros((SSD_G, SSD_N, SSD_GW), F32), jnp.zeros((TAIL_ROWS, SSD_XBC), F32))
    zero_hg = jnp.zeros((HG_HEADS // 2, 2 * HG_K, 2 * HG_K), F32)
    _, st_ssd, tail = _ssd_call(*(r3(t, 1, n_meta) for t in proj_m[:3]), _ssd_consts(*ssd_p, n_meta), *zero_ssd,
                                q=n_meta, n_sub=1)
    _, st_hg = _hgrn2_call(*(r3(t, 1, n_meta) for t in proj_m[3:7]), *hg_p, zero_hg, q=n_meta, n_sub=1, layer=0)

    tl = MAIN_TILING
    x2 = x.reshape(bsz * seq, d)
    h1, u = _ffn1_call(x2, row(ffn1_norm[0]), wgu1, wd1, row(mix_norm[0]), tl.rows)
    ya = _projssd_call(u, *w_ssd, _ssd_consts(*ssd_p, tl.chunk), st_ssd[0], tail[0], bsz, seq,
                       q=tl.chunk, n_sub=tl.ssd_sub)
    hq, hf, hi, hg, ga, gb = _inproj_call(u, w_rest, rest_dtypes, tl.rows)
    yb, _ = _hgrn2_call(*(r3(t, bsz, seq) for t in (hq, hf, hi, hg)), *hg_p, st_hg[0],
                        q=tl.chunk, n_sub=tl.hg_sub, layer=0)
    out = _tail_call(h1, ya, yb.reshape(bsz * seq, -1), ga, gb,
                     w_branch_a[0].astype(BF16), w_branch_b[0].astype(BF16), w_out[0].astype(BF16),
                     row(ffn2_norm[0]), wgu2, wd2, row(final_norm), tl.rows)
    return out.reshape(bsz, seq, d)
```

```python
import functools
from typing import NamedTuple

import numpy as np
import jax
import jax.numpy as jnp
from jax import lax
from jax.experimental import pallas as pl
from jax.experimental.pallas import tpu as pltpu

F32 = jnp.float32
BF16 = jnp.bfloat16
EPS = 1e-6

D_MODEL = 1024
D_FF = 2816
FF_CHUNK = 256
N_FF_CHUNKS = D_FF // FF_CHUNK
PROJ_CHUNK = 256

SSD_HEADS = 16
SSD_P = 64
SSD_G = 4
SSD_R = SSD_HEADS // SSD_G
SSD_N = 128
SSD_INNER = SSD_HEADS * SSD_P
SSD_CONV = 4
SSD_XBC = SSD_INNER + 2 * SSD_G * SSD_N
SSD_GW = SSD_R * SSD_P
LANES = 128
TAIL_ROWS = 8
CUMSUM_TERMS = 3
EXPAND_TERMS = 2

HG_HEADS = 8
HG_K = 128
HG_WIDTH = HG_HEADS * HG_K

VMEM_LIMIT_BYTES = 56 * 1024 * 1024


def _params(sem):
    return pltpu.CompilerParams(dimension_semantics=sem, vmem_limit_bytes=VMEM_LIMIT_BYTES)


def _const_spec(shape):
    zeros = (0,) * len(shape)
    return pl.BlockSpec(shape, lambda *_: zeros, pipeline_mode=pl.Buffered(1))


def _rms(x, w):
    return x * lax.rsqrt(jnp.mean(x * x, axis=-1, keepdims=True) + EPS) * w


def _sigmoid(x):
    return 0.5 * jnp.tanh(0.5 * x) + 0.5


def _silu(x):
    return x * _sigmoid(x)


def _split_terms(x, n):
    terms, rest = [], x
    for i in range(n):
        terms.append(rest.astype(BF16))
        if i + 1 < n:
            rest = rest - terms[-1].astype(F32)
    return terms


def _dot(a, b):
    return jnp.dot(a, b, preferred_element_type=F32)


def _dot_nt(a, b):
    return lax.dot_general(a, b, (((1,), (1,)), ((), ())), preferred_element_type=F32)


def _dot_tn(a, b):
    return lax.dot_general(a, b, (((0,), (0,)), ((), ())), preferred_element_type=F32)


def _stack(mat, n, axis):
    return np.concatenate([mat] * n, axis=axis)


def _merged(*stage_generators):
    live = list(stage_generators)
    while live:
        for gen in list(live):
            if next(gen, StopIteration) is StopIteration:
                live.remove(gen)
        yield


def _run(*stage_generators):
    for _ in _merged(*stage_generators):
        pass


def _ffn_stages(xn, wgu_ref, wd_ref, acc_ref):
    for c in range(N_FF_CHUNKS):
        g = _dot(xn, wgu_ref[:, c * FF_CHUNK:(c + 1) * FF_CHUNK])
        u = _dot(xn, wgu_ref[:, D_FF + c * FF_CHUNK:D_FF + (c + 1) * FF_CHUNK])
        d = _dot((_silu(g) * u).astype(BF16), wd_ref[c * FF_CHUNK:(c + 1) * FF_CHUNK, :])
        if c == 0:
            acc_ref[...] = d
        else:
            acc_ref[...] += d
        yield


def _ffn1_body(x_ref, nw_ref, wgu_ref, wd_ref, nw2_ref, h_ref, u_ref, acc_ref):
    x = x_ref[...]
    xn = _rms(x, nw_ref[...]).astype(BF16)
    _run(_ffn_stages(xn, wgu_ref, wd_ref, acc_ref))
    h = x + acc_ref[...]
    h_ref[...] = h
    u_ref[...] = _rms(h, nw2_ref[...]).astype(BF16)


def _ffn1_call(x2, nw, wgu, wd, nw2, tm):
    m = x2.shape[0]
    row = lambda i: (i, 0)
    return pl.pallas_call(
        _ffn1_body,
        grid=(m // tm,),
        in_specs=[
            pl.BlockSpec((tm, D_MODEL), row),
            _const_spec((1, D_MODEL)),
            _const_spec(wgu.shape), _const_spec(wd.shape),
            _const_spec((1, D_MODEL)),
        ],
        out_specs=[pl.BlockSpec((tm, D_MODEL), row), pl.BlockSpec((tm, D_MODEL), row)],
        out_shape=[jax.ShapeDtypeStruct((m, D_MODEL), F32), jax.ShapeDtypeStruct((m, D_MODEL), BF16)],
        scratch_shapes=[pltpu.VMEM((tm, D_MODEL), F32)],
        compiler_params=_params(("arbitrary",)),
        name="ffn1",
    )(x2, nw, wgu, wd, nw2)


def _proj_stages(u_ref, w_refs, o_refs):
    u = u_ref[...]
    for w_ref, o_ref in zip(w_refs, o_refs):
        width = w_ref.shape[1]
        step = min(PROJ_CHUNK, width)
        for lo in range(0, width, step):
            o_ref[:, lo:lo + step] = _dot(u, w_ref[:, lo:lo + step]).astype(o_ref.dtype)
            yield


def _inproj_body(u_ref, *refs):
    n = len(refs) // 2
    _run(_proj_stages(u_ref, refs[:n], refs[n:]))


def _inproj_call(u2, weights, out_dtypes, tm):
    m = u2.shape[0]
    row = lambda i: (i, 0)
    return pl.pallas_call(
        _inproj_body,
        grid=(m // tm,),
        in_specs=[pl.BlockSpec((tm, D_MODEL), row)] + [_const_spec(w.shape) for w in weights],
        out_specs=[pl.BlockSpec((tm, w.shape[1]), row) for w in weights],
        out_shape=[jax.ShapeDtypeStruct((m, w.shape[1]), dt) for w, dt in zip(weights, out_dtypes)],
        compiler_params=_params(("arbitrary",)),
        name="inproj",
    )(u2, *weights)


class _SsdConsts(NamedTuple):
    cw: jax.Array
    cb: jax.Array
    dtb: jax.Array
    alog: jax.Array
    dsk: jax.Array
    nw: jax.Array
    lincl: jax.Array
    eye: jax.Array
    expand: jax.Array
    hmask: jax.Array


def _ssd_consts(cw, cb, dtb, alog, dsk, nw, q):
    expand = np.zeros((LANES, SSD_INNER), np.float32)
    hmask = np.zeros((SSD_R, q, SSD_INNER), np.float32)
    for h in range(SSD_HEADS):
        expand[h, h * SSD_P:(h + 1) * SSD_P] = 1.0
        hmask[h % SSD_R, :, h * SSD_P:(h + 1) * SSD_P] = 1.0
    return _SsdConsts(
        cw, cb, dtb, alog, dsk, nw,
        jnp.asarray(_stack(np.tril(np.ones((q, q), np.float32)), CUMSUM_TERMS, 1), BF16),
        jnp.asarray(_stack(np.eye(LANES, dtype=np.float32), CUMSUM_TERMS, 1), BF16),
        jnp.asarray(_stack(expand, EXPAND_TERMS, 0), BF16),
        jnp.asarray(hmask, BF16))


def _ssd_stages(z_ref, xbc_ref, dt_ref, k, y_ref, st_scr, xpad_scr, xbd_scr, *, q, n_sub):
    t = n_sub * q
    xpad_scr[TAIL_ROWS:TAIL_ROWS + t, :] = xbc_ref[...].astype(F32)
    neg_a = -jnp.exp(k.alog[...])
    li = lax.broadcasted_iota(jnp.int32, (q, q), 0)
    si = lax.broadcasted_iota(jnp.int32, (q, q), 1)
    causal = li >= si

    for s in range(n_sub):
        rows = slice(s * q, (s + 1) * q)
        conv = k.cb[...]
        for tap in reversed(range(SSD_CONV)):
            lo = TAIL_ROWS - (SSD_CONV - 1) + tap + s * q
            conv = conv + k.cw[tap:tap + 1, :] * xpad_scr[lo:lo + q, :]
        xc = _silu(conv)
        xs = xc[:, :SSD_INNER]
        bm = xc[:, SSD_INNER:SSD_INNER + SSD_G * SSD_N]
        cm = xc[:, SSD_INNER + SSD_G * SSD_N:]
        dt = jax.nn.softplus(dt_ref[rows, :] + k.dtb[...])
        a_cs = _dot(k.lincl[...], jnp.concatenate(_split_terms(dt * neg_a, CUMSUM_TERMS), axis=0))
        a_last = a_cs[q - 1:q, :]
        a_cs_t = _dot_nt(k.eye[...], jnp.concatenate(_split_terms(a_cs, CUMSUM_TERMS), axis=1))
        cols = jnp.concatenate([dt, jnp.exp(a_cs), dt * jnp.exp(a_last - a_cs)], axis=0)
        cols_x = _dot(jnp.concatenate(_split_terms(cols, EXPAND_TERMS), axis=1), k.expand[...])
        dt_x, ea_x, dtdec_x = cols_x[0:q], cols_x[q:2 * q], cols_x[2 * q:3 * q]
        x_dt = (xs * dt_x).astype(BF16)
        x_dec = (xs * dtdec_x).astype(BF16)
        x_heads = [x_dt * k.hmask[r] for r in range(SSD_R)]
        yield
        for g in range(SSD_G):
            gl = slice(g * SSD_GW, (g + 1) * SSD_GW)
            b_g = bm[:, g * SSD_N:(g + 1) * SSD_N].astype(BF16)
            c_g = cm[:, g * SSD_N:(g + 1) * SSD_N].astype(BF16)
            cbm = jnp.where(causal, _dot_nt(c_g, b_g), 0.0)
            m_cat = jnp.concatenate(
                [(cbm * jnp.exp(jnp.minimum(a_cs[:, h:h + 1] - a_cs_t[h:h + 1, :], 0.0))).astype(BF16)
                 for h in range(g * SSD_R, (g + 1) * SSD_R)], axis=1)
            for r in range(SSD_R):
                xbd_scr[r * q:(r + 1) * q, :] = x_heads[r][:, gl]
            st = st_scr[g]
            y = _dot(m_cat, xbd_scr[...]) + _dot(c_g, st.astype(BF16)) * ea_x[:, gl] + xs[:, gl] * k.dsk[:, gl]
            st_scr[g] = st * ea_x[q - 1:q, gl] + _dot_tn(b_g, x_dec[:, gl])
            yg = y * _silu(z_ref[rows, gl].astype(F32))
            yg = yg * lax.rsqrt(jnp.mean(yg * yg, axis=-1, keepdims=True) + EPS)
            y_ref[rows, gl] = (yg * k.nw[:, gl]).astype(y_ref.dtype)
            yield
    xpad_scr[TAIL_ROWS - (SSD_CONV - 1):TAIL_ROWS, :] = xpad_scr[TAIL_ROWS + t - (SSD_CONV - 1):TAIL_ROWS + t, :]
    yield


N_SSD_CONSTS = len(_SsdConsts._fields)


def _ssd_body(z_ref, xbc_ref, dt_ref, *refs, q, n_sub):
    k = _SsdConsts(*refs[:N_SSD_CONSTS])
    st0_ref, tail0_ref, y_ref, stout_ref, tailout_ref, st_scr, xpad_scr, xbd_scr = refs[N_SSD_CONSTS:]
    c = pl.program_id(1)

    @pl.when(c == 0)
    def _():
        st_scr[...] = st0_ref[...]
        xpad_scr[0:TAIL_ROWS, :] = tail0_ref[...]

    _run(_ssd_stages(z_ref.at[0], xbc_ref.at[0], dt_ref.at[0], k, y_ref.at[0], st_scr, xpad_scr, xbd_scr,
                     q=q, n_sub=n_sub))

    @pl.when(c == pl.num_programs(1) - 1)
    def _():
        stout_ref[0] = st_scr[...]
        tailout_ref[0] = xpad_scr[0:TAIL_ROWS, :]


def _ssd_scratch(q, t):
    return [pltpu.VMEM((SSD_G, SSD_N, SSD_GW), F32),
            pltpu.VMEM((TAIL_ROWS + t, SSD_XBC), F32),
            pltpu.VMEM((SSD_R * q, SSD_GW), BF16)]


def _ssd_call(z, xbc, dt, consts, st0, tail0, q, n_sub):
    bsz, seq, _ = z.shape
    t = q * n_sub
    assert seq % t == 0 and t >= SSD_CONV - 1
    blk = lambda b, c: (b, c, 0)
    per_b3 = lambda b, c: (b, 0, 0)
    per_b4 = lambda b, c: (b, 0, 0, 0)
    return pl.pallas_call(
        functools.partial(_ssd_body, q=q, n_sub=n_sub),
        grid=(bsz, seq // t),
        in_specs=[
            pl.BlockSpec((1, t, SSD_INNER), blk),
            pl.BlockSpec((1, t, SSD_XBC), blk),
            pl.BlockSpec((1, t, LANES), blk),
        ] + [_const_spec(c.shape) for c in consts] + [_const_spec(st0.shape), _const_spec(tail0.shape)],
        out_specs=[
            pl.BlockSpec((1, t, SSD_INNER), blk),
            pl.BlockSpec((1, SSD_G, SSD_N, SSD_GW), per_b4),
            pl.BlockSpec((1, TAIL_ROWS, SSD_XBC), per_b3),
        ],
        out_shape=[
            jax.ShapeDtypeStruct((bsz, seq, SSD_INNER), BF16),
            jax.ShapeDtypeStruct((bsz, SSD_G, SSD_N, SSD_GW), F32),
            jax.ShapeDtypeStruct((bsz, TAIL_ROWS, SSD_XBC), F32),
        ],
        scratch_shapes=_ssd_scratch(q, t),
        compiler_params=_params(("arbitrary", "arbitrary")),
        name="ssd",
    )(z, xbc, dt, *consts, st0, tail0)


def _projssd_body(u_ref, wz_ref, wx_ref, wdt_ref, *refs, q, n_sub, blocks_per_seq):
    k = _SsdConsts(*refs[:N_SSD_CONSTS])
    st0_ref, tail0_ref, y_ref, z_scr, xbc_scr, dt_scr, st_scr, xpad_scr, xbd_scr = refs[N_SSD_CONSTS:]
    i = pl.program_id(0)

    @pl.when((i == 0) | ((i + blocks_per_seq - 1) % blocks_per_seq == 0))
    def _():
        st_scr[...] = st0_ref[...]
        xpad_scr[0:TAIL_ROWS, :] = tail0_ref[...]

    @pl.when(i == 0)
    def _():
        for scr in (z_scr, xbc_scr, dt_scr):
            scr[1] = jnp.zeros(scr.shape[1:], scr.dtype)

    slot = i % 2
    _run(_ssd_stages(z_scr.at[1 - slot], xbc_scr.at[1 - slot], dt_scr.at[1 - slot], k, y_ref,
                     st_scr, xpad_scr, xbd_scr, q=q, n_sub=n_sub),
         _proj_stages(u_ref, (wz_ref, wx_ref, wdt_ref), (z_scr.at[slot], xbc_scr.at[slot], dt_scr.at[slot])))


def _projssd_call(u2, wz, wx, wdt, consts, st0, tail0, bsz, seq, q, n_sub):
    t = q * n_sub
    assert seq % t == 0 and t >= SSD_CONV - 1
    blocks_per_seq = seq // t
    n_blocks = bsz * blocks_per_seq
    cur = lambda i: (jnp.minimum(i, n_blocks - 1), 0)
    prev = lambda i: (jnp.maximum(i - 1, 0), 0)
    fixed = (wz, wx, wdt) + tuple(consts) + (st0, tail0)
    return pl.pallas_call(
        functools.partial(_projssd_body, q=q, n_sub=n_sub, blocks_per_seq=blocks_per_seq),
        grid=(n_blocks + 1,),
        in_specs=[pl.BlockSpec((t, D_MODEL), cur)] + [_const_spec(c.shape) for c in fixed],
        out_specs=pl.BlockSpec((t, SSD_INNER), prev),
        out_shape=jax.ShapeDtypeStruct((bsz * seq, SSD_INNER), BF16),
        scratch_shapes=[pltpu.VMEM((2, t, SSD_INNER), BF16), pltpu.VMEM((2, t, SSD_XBC), BF16),
                        pltpu.VMEM((2, t, LANES), F32)] + _ssd_scratch(q, t),
        compiler_params=_params(("arbitrary",)),
        name="proj_ssd",
    )(u2, *fixed)


def _hg_levels(q):
    out, m = [], q // 2
    while m >= 1:
        out.append(m)
        m //= 2
    return out


SUBLANES = 8


def _hg_matmul_levels(q):
    return [m for m in _hg_levels(q) if 2 * m < SUBLANES]


def _hg_sum_matrices(q):
    t = np.arange(q)
    l, tt = t[:, None], t[None, :]
    mats = [tt <= l]
    for m in _hg_matmul_levels(q):
        ref = (l // (2 * m)) * (2 * m) + m - 1
        lower = (l % (2 * m)) >= m
        mats.append(np.where(lower, (tt > ref) & (tt <= l), (tt > l) & (tt <= ref)))
    return np.concatenate(mats, 0).astype(np.float32)


def _hg_level_log_decays(gc, sums, q):
    small = _hg_matmul_levels(q)
    w = gc.shape[-1]
    out = []
    for m in _hg_levels(q):
        if m in small:
            j = 1 + small.index(m)
            out.append(sums[j * q:(j + 1) * q])
        elif m >= SUBLANES:
            g4 = gc.reshape(q // (2 * m), 2, m, w)
            upper, lower = g4[:, 0], g4[:, 1]
            ref = upper[:, m - 1:m, :]
            out.append(jnp.stack([ref - upper, lower - ref], axis=1).reshape(q, w))
        else:
            g3 = gc.reshape(q // (2 * m), 2 * m, w)
            d = g3 - g3[:, m - 1:m, :]
            out.append(jnp.minimum(d, -d).reshape(q, w))
    return out


def _hg_level_masks(q):
    t = np.arange(q)
    l, s = t[:, None], t[None, :]
    masks = [(l > s) & (((l ^ s) >> int(np.log2(m))) == 1) for m in _hg_levels(q)] + [l == s]
    return np.stack([np.concatenate([mk, mk], 1) for mk in masks]).astype(np.float32)


def _block_diag_pair(x):
    z = jnp.zeros((x.shape[0], HG_K), x.dtype)
    return jnp.concatenate([jnp.concatenate([x[:, :HG_K], z], 1), jnp.concatenate([z, x[:, HG_K:]], 1)], 0)


def _hgrn2_stages(q_ref, f_ref, i_ref, g_ref, lbp_ref, nw_ref, sums_ref, masks_ref, y_ref, st_scr,
                  *, q, n_sub, layer):
    lbp = lbp_ref[...]
    e = jnp.exp(lbp - jnp.max(lbp, axis=0, keepdims=True))
    lb = jnp.sum(e[:layer + 1], axis=0, keepdims=True) / jnp.sum(e, axis=0, keepdims=True)

    n_lev = len(_hg_levels(q))

    def chunk(s):
        rows = slice(s * q, (s + 1) * q)
        f = lb + (1.0 - lb) * _sigmoid(f_ref[rows, :].astype(F32))
        kk_b = (1.0 - f).astype(BF16)
        qs_b = _silu(q_ref[rows, :].astype(F32)).astype(BF16)
        v = i_ref[rows, :]
        lf = jnp.log(f) * np.float32(np.log2(np.e))
        sums = _dot(sums_ref[...], jnp.concatenate(_split_terms(lf, EXPAND_TERMS), axis=0))
        gc = sums[0:q]
        p_inc = jnp.exp2(gc)
        p_last = p_inc[q - 1:q]
        p_lev = [jnp.exp2(e).astype(BF16) for e in _hg_level_log_decays(gc, sums, q)]
        wq = [qs_b * p for p in p_lev] + [qs_b]
        wk = [kk_b * p for p in p_lev] + [kk_b]
        qg = qs_b * p_inc.astype(BF16)
        kd = kk_b * jnp.exp2(gc[q - 1:q] - gc).astype(BF16)
        yield

        for pair in range(HG_HEADS // 2):
            pl2 = slice(2 * pair * HG_K, 2 * (pair + 1) * HG_K)
            att = jnp.zeros((q, 2 * q), BF16)
            for j in range(n_lev + 1):
                a_j = _dot_nt(wq[j][:, pl2], _block_diag_pair(wk[j][:, pl2]))
                att = att + a_j.astype(BF16) * masks_ref[j]
            st = st_scr[pair]
            o = _dot(att, _block_diag_pair(v[:, pl2])) + _dot_nt(qg[:, pl2], st.astype(BF16))
            for hh in range(2):
                h = 2 * pair + hh
                hl = slice(h * HG_K, (h + 1) * HG_K)
                bl = slice(hh * HG_K, (hh + 1) * HG_K)
                st_scr[pair, bl, bl] = st[bl, bl] * p_last[:, hl] + _dot_tn(v[:, hl], kd[:, hl])
                o_h = o[:, bl]
                o_h = o_h * lax.rsqrt(jnp.mean(o_h * o_h, axis=-1, keepdims=True) + EPS) * nw_ref[:, hl]
                y_ref[rows, hl] = (o_h * _silu(g_ref[rows, hl].astype(F32))).astype(y_ref.dtype)
            yield

    for s in range(n_sub):
        yield from chunk(s)


def _hgrn2_body(q_ref, f_ref, i_ref, g_ref, lbp_ref, nw_ref, sums_ref, masks_ref, st0_ref,
                y_ref, stout_ref, st_scr, *, q, n_sub, layer):
    c = pl.program_id(1)

    @pl.when(c == 0)
    def _():
        st_scr[...] = st0_ref[...]

    _run(_hgrn2_stages(q_ref.at[0], f_ref.at[0], i_ref.at[0], g_ref.at[0], lbp_ref, nw_ref, sums_ref, masks_ref,
                       y_ref.at[0], st_scr, q=q, n_sub=n_sub, layer=layer))

    @pl.when(c == pl.num_programs(1) - 1)
    def _():
        stout_ref[0] = st_scr[...]


def _hgrn2_call(qq, ff, ii, gg, lbp, nw, st0, q, n_sub, layer):
    bsz, seq, _ = qq.shape
    t = q * n_sub
    assert seq % t == 0 and q & (q - 1) == 0
    sums = jnp.asarray(_stack(_hg_sum_matrices(q), EXPAND_TERMS, 1), BF16)
    masks = jnp.asarray(_hg_level_masks(q), BF16)
    blk = lambda b, c: (b, c, 0)
    per_b4 = lambda b, c: (b, 0, 0, 0)
    tok = pl.BlockSpec((1, t, HG_WIDTH), blk)
    st_shape = (HG_HEADS // 2, 2 * HG_K, 2 * HG_K)
    return pl.pallas_call(
        functools.partial(_hgrn2_body, q=q, n_sub=n_sub, layer=layer),
        grid=(bsz, seq // t),
        in_specs=[tok, tok, tok, tok, _const_spec(lbp.shape), _const_spec(nw.shape),
                  _const_spec(sums.shape), _const_spec(masks.shape), _const_spec(st0.shape)],
        out_specs=[tok, pl.BlockSpec((1,) + st_shape, per_b4)],
        out_shape=[jax.ShapeDtypeStruct((bsz, seq, HG_WIDTH), BF16),
                   jax.ShapeDtypeStruct((bsz,) + st_shape, F32)],
        scratch_shapes=[pltpu.VMEM(st_shape, F32)],
        compiler_params=_params(("arbitrary", "arbitrary")),
        name="hgrn2",
    )(qq, ff, ii, gg, lbp, nw, sums, masks, st0)


def _tail_body(h_ref, ya_ref, yb_ref, ga_ref, gb_ref, wa_ref, wb_ref, wo_ref,
               nw_ref, wgu_ref, wd_ref, fnw_ref, o_ref, acc_ref):
    merged = (_sigmoid(ga_ref[...].astype(F32)) * _dot(ya_ref[...], wa_ref[...])
              + _sigmoid(gb_ref[...].astype(F32)) * _dot(yb_ref[...], wb_ref[...]))
    h = h_ref[...] + _dot(merged.astype(BF16), wo_ref[...])
    xn = _rms(h, nw_ref[...]).astype(BF16)
    _run(_ffn_stages(xn, wgu_ref, wd_ref, acc_ref))
    h = h + acc_ref[...]
    o_ref[...] = _rms(h, fnw_ref[...])


def _tail_call(h, ya, yb, ga, gb, wa, wb, wo, nw, wgu, wd, fnw, tm):
    m = h.shape[0]
    row = lambda i: (i, 0)
    tile = pl.BlockSpec((tm, D_MODEL), row)
    return pl.pallas_call(
        _tail_body,
        grid=(m // tm,),
        in_specs=[tile, tile, tile, tile, tile,
                  _const_spec(wa.shape), _const_spec(wb.shape), _const_spec(wo.shape),
                  _const_spec((1, D_MODEL)),
                  _const_spec(wgu.shape), _const_spec(wd.shape),
                  _const_spec((1, D_MODEL))],
        out_specs=tile,
        out_shape=jax.ShapeDtypeStruct((m, D_MODEL), F32),
        scratch_shapes=[pltpu.VMEM((tm, D_MODEL), F32)],
        compiler_params=_params(("arbitrary",)),
        name="tail",
    )(h, ya, yb, ga, gb, wa, wb, wo, nw, wgu, wd, fnw)


def _ffn_weights(w_gu, w_down):
    return w_gu.astype(BF16), (0.5 * w_down).astype(BF16)


def _pad_lanes(v):
    return jnp.zeros((1, LANES), F32).at[0, :v.shape[0]].set(v.astype(F32))


class _Tiling(NamedTuple):
    rows: int
    chunk: int
    ssd_sub: int
    hg_sub: int


MAIN_TILING = _Tiling(rows=512, chunk=128, ssd_sub=4, hg_sub=8)


def kernel(x, meta_tokens, ffn1_norm, ffn1_w_gu, ffn1_w_down, mix_norm, w_in, ssd_conv_w, ssd_conv_b,
           ssd_dt_bias, ssd_a_log, ssd_d, ssd_norm, hg_lower_bound, hg_norm, w_branch_a, w_branch_b,
           w_out, ffn2_norm, ffn2_w_gu, ffn2_w_down, final_norm):
    bsz, seq, d = x.shape
    assert d == D_MODEL and ffn1_norm.shape[0] == 1, "single-layer block of width D_MODEL only"
    n_meta = meta_tokens.shape[0]
    row = lambda v: v.reshape(1, -1).astype(F32)
    r3 = lambda t, b, s: t.reshape(b, s, t.shape[-1])

    wgu1, wd1 = _ffn_weights(ffn1_w_gu[0], ffn1_w_down[0])
    wgu2, wd2 = _ffn_weights(ffn2_w_gu[0], ffn2_w_down[0])
    sizes = (SSD_INNER, SSD_XBC, SSD_HEADS, HG_WIDTH, HG_WIDTH, HG_WIDTH, HG_WIDTH, D_MODEL, D_MODEL)
    offs = np.cumsum((0,) + sizes)
    w_bf = lax.optimization_barrier(w_in[0].astype(BF16))
    w_parts = [w_bf[:, offs[j]:offs[j + 1]] for j in range(len(sizes))]
    w_parts[2] = jnp.pad(w_parts[2], ((0, 0), (0, LANES - SSD_HEADS)))
    w_ssd, w_rest = w_parts[:3], w_parts[3:]
    ssd_dtypes, rest_dtypes = [BF16, BF16, F32], [BF16] * 6

    ssd_p = (ssd_conv_w[0].astype(F32), row(ssd_conv_b[0]), _pad_lanes(ssd_dt_bias[0]), _pad_lanes(ssd_a_log[0]),
             row(jnp.repeat(ssd_d[0], SSD_P)), row(ssd_norm[0]))
    hg_p = (hg_lower_bound.astype(F32), row(hg_norm[0]))

    _, u_m = _ffn1_call(meta_tokens.astype(F32), row(ffn1_norm[0]), wgu1, wd1, row(mix_norm[0]), n_meta)
    proj_m = _inproj_call(u_m, w_parts, ssd_dtypes + rest_dtypes, n_meta)
    zero_ssd = (jnp.zeros((SSD_G, SSD_N, SSD_GW), F32), jnp.zeros((TAIL_ROWS, SSD_XBC), F32))
    zero_hg = jnp.zeros((HG_HEADS // 2, 2 * HG_K, 2 * HG_K), F32)
    _, st_ssd, tail = _ssd_call(*(r3(t, 1, n_meta) for t in proj_m[:3]), _ssd_consts(*ssd_p, n_meta), *zero_ssd,
                                q=n_meta, n_sub=1)
    _, st_hg = _hgrn2_call(*(r3(t, 1, n_meta) for t in proj_m[3:7]), *hg_p, zero_hg, q=n_meta, n_sub=1, layer=0)

    tl = MAIN_TILING
    x2 = x.reshape(bsz * seq, d)
    h1, u = _ffn1_call(x2, row(ffn1_norm[0]), wgu1, wd1, row(mix_norm[0]), tl.rows)
    ya = _projssd_call(u, *w_ssd, _ssd_consts(*ssd_p, tl.chunk), st_ssd[0], tail[0], bsz, seq,
                       q=tl.chunk, n_sub=tl.ssd_sub)
    hq, hf, hi, hg, ga, gb = _inproj_call(u, w_rest, rest_dtypes, tl.rows)
    yb, _ = _hgrn2_call(*(r3(t, bsz, seq) for t in (hq, hf, hi, hg)), *hg_p, st_hg[0],
                        q=tl.chunk, n_sub=tl.hg_sub, layer=0)
    out = _tail_call(h1, ya, yb.reshape(bsz * seq, -1), ga, gb,
                     w_branch_a[0].astype(BF16), w_branch_b[0].astype(BF16), w_out[0].astype(BF16),
                     row(ffn2_norm[0]), wgu2, wd2, row(final_norm), tl.rows)
    return out.reshape(bsz, seq, d)
```

```python
import functools
from typing import NamedTuple

import numpy as np
import jax
import jax.numpy as jnp
from jax import lax
from jax.experimental import pallas as pl
from jax.experimental.pallas import tpu as pltpu

F32 = jnp.float32
BF16 = jnp.bfloat16
EPS = 1e-6

D_MODEL = 1024
D_FF = 2816
FF_CHUNK = 256
N_FF_CHUNKS = D_FF // FF_CHUNK
PROJ_CHUNK = 256

SSD_HEADS = 16
SSD_P = 64
SSD_G = 4
SSD_R = SSD_HEADS // SSD_G
SSD_N = 128
SSD_INNER = SSD_HEADS * SSD_P
SSD_CONV = 4
SSD_XBC = SSD_INNER + 2 * SSD_G * SSD_N
SSD_GW = SSD_R * SSD_P
LANES = 128
TAIL_ROWS = 8
CUMSUM_TERMS = 3
EXPAND_TERMS = 2

HG_HEADS = 8
HG_K = 128
HG_WIDTH = HG_HEADS * HG_K

VMEM_LIMIT_BYTES = 56 * 1024 * 1024


def _params(sem):
    return pltpu.CompilerParams(dimension_semantics=sem, vmem_limit_bytes=VMEM_LIMIT_BYTES)


def _const_spec(shape):
    zeros = (0,) * len(shape)
    return pl.BlockSpec(shape, lambda *_: zeros, pipeline_mode=pl.Buffered(1))


def _rms(x, w):
    return x * lax.rsqrt(jnp.mean(x * x, axis=-1, keepdims=True) + EPS) * w


def _sigmoid(x):
    return 0.5 * jnp.tanh(0.5 * x) + 0.5


def _silu(x):
    return x * _sigmoid(x)


def _split_terms(x, n):
    terms, rest = [], x
    for i in range(n):
        terms.append(rest.astype(BF16))
        if i + 1 < n:
            rest = rest - terms[-1].astype(F32)
    return terms


def _dot(a, b):
    return jnp.dot(a, b, preferred_element_type=F32)


def _dot_nt(a, b):
    return lax.dot_general(a, b, (((1,), (1,)), ((), ())), preferred_element_type=F32)


def _dot_tn(a, b):
    return lax.dot_general(a, b, (((0,), (0,)), ((), ())), preferred_element_type=F32)


def _stack(mat, n, axis):
    return np.concatenate([mat] * n, axis=axis)


def _merged(*stage_generators):
    live = list(stage_generators)
    while live:
        for gen in list(live):
            if next(gen, StopIteration) is StopIteration:
                live.remove(gen)
        yield


def _run(*stage_generators):
    for _ in _merged(*stage_generators):
        pass


def _ffn_stages(xn, wgu_ref, wd_ref, acc_ref):
    for c in range(N_FF_CHUNKS):
        g = _dot(xn, wgu_ref[:, c * FF_CHUNK:(c + 1) * FF_CHUNK])
        u = _dot(xn, wgu_ref[:, D_FF + c * FF_CHUNK:D_FF + (c + 1) * FF_CHUNK])
        d = _dot((_silu(g) * u).astype(BF16), wd_ref[c * FF_CHUNK:(c + 1) * FF_CHUNK, :])
        if c == 0:
            acc_ref[...] = d
        else:
            acc_ref[...] += d
        yield


def _ffn1_body(x_ref, nw_ref, wgu_ref, wd_ref, nw2_ref, h_ref, u_ref, acc_ref):
    x = x_ref[...]
    xn = _rms(x, nw_ref[...]).astype(BF16)
    _run(_ffn_stages(xn, wgu_ref, wd_ref, acc_ref))
    h = x + acc_ref[...]
    h_ref[...] = h
    u_ref[...] = _rms(h, nw2_ref[...]).astype(BF16)


def _ffn1_call(x2, nw, wgu, wd, nw2, tm):
    m = x2.shape[0]
    row = lambda i: (i, 0)
    return pl.pallas_call(
        _ffn1_body,
        grid=(m // tm,),
        in_specs=[
            pl.BlockSpec((tm, D_MODEL), row),
            _const_spec((1, D_MODEL)),
            _const_spec(wgu.shape), _const_spec(wd.shape),
            _const_spec((1, D_MODEL)),
        ],
        out_specs=[pl.BlockSpec((tm, D_MODEL), row), pl.BlockSpec((tm, D_MODEL), row)],
        out_shape=[jax.ShapeDtypeStruct((m, D_MODEL), F32), jax.ShapeDtypeStruct((m, D_MODEL), BF16)],
        scratch_shapes=[pltpu.VMEM((tm, D_MODEL), F32)],
        compiler_params=_params(("arbitrary",)),
        name="ffn1",
    )(x2, nw, wgu, wd, nw2)


def _proj_stages(u_ref, wt_refs, o_refs):
    u = u_ref[...]
    for wt_ref, o_ref in zip(wt_refs, o_refs):
        width = wt_ref.shape[0]
        step = min(PROJ_CHUNK, width)
        for lo in range(0, width, step):
            o_ref[:, lo:lo + step] = _dot_nt(u, wt_ref[lo:lo + step, :]).astype(o_ref.dtype)
            yield


def _inproj_body(u_ref, *refs):
    n = len(refs) // 2
    _run(_proj_stages(u_ref, refs[:n], refs[n:]))


def _inproj_call(u2, weights, out_dtypes, tm):
    m = u2.shape[0]
    row = lambda i: (i, 0)
    return pl.pallas_call(
        _inproj_body,
        grid=(m // tm,),
        in_specs=[pl.BlockSpec((tm, D_MODEL), row)] + [_const_spec(w.shape) for w in weights],
        out_specs=[pl.BlockSpec((tm, w.shape[0]), row) for w in weights],
        out_shape=[jax.ShapeDtypeStruct((m, w.shape[0]), dt) for w, dt in zip(weights, out_dtypes)],
        compiler_params=_params(("arbitrary",)),
        name="inproj",
    )(u2, *weights)


class _SsdConsts(NamedTuple):
    cw: jax.Array
    cb: jax.Array
    dtb: jax.Array
    alog: jax.Array
    dsk: jax.Array
    nw: jax.Array
    lincl: jax.Array
    eye: jax.Array
    expand: jax.Array
    hmask: jax.Array


def _ssd_consts(cw, cb, dtb, alog, dsk, nw, q):
    expand = np.zeros((LANES, SSD_INNER), np.float32)
    hmask = np.zeros((SSD_R, q, SSD_INNER), np.float32)
    for h in range(SSD_HEADS):
        expand[h, h * SSD_P:(h + 1) * SSD_P] = 1.0
        hmask[h % SSD_R, :, h * SSD_P:(h + 1) * SSD_P] = 1.0
    return _SsdConsts(
        cw, cb, dtb, alog, dsk, nw,
        jnp.asarray(_stack(np.tril(np.ones((q, q), np.float32)), CUMSUM_TERMS, 1), BF16),
        jnp.asarray(_stack(np.eye(LANES, dtype=np.float32), CUMSUM_TERMS, 1), BF16),
        jnp.asarray(_stack(expand, EXPAND_TERMS, 0), BF16),
        jnp.asarray(hmask, BF16))


def _ssd_stages(z_ref, xbc_ref, dt_ref, k, y_ref, st_scr, xpad_scr, xbd_scr, *, q, n_sub):
    t = n_sub * q
    xpad_scr[TAIL_ROWS:TAIL_ROWS + t, :] = xbc_ref[...].astype(F32)
    neg_a = -jnp.exp(k.alog[...])
    li = lax.broadcasted_iota(jnp.int32, (q, q), 0)
    si = lax.broadcasted_iota(jnp.int32, (q, q), 1)
    causal = li >= si

    for s in range(n_sub):
        rows = slice(s * q, (s + 1) * q)
        conv = k.cb[...]
        for tap in reversed(range(SSD_CONV)):
            lo = TAIL_ROWS - (SSD_CONV - 1) + tap + s * q
            conv = conv + k.cw[tap:tap + 1, :] * xpad_scr[lo:lo + q, :]
        xc = _silu(conv)
        xs = xc[:, :SSD_INNER]
        bm = xc[:, SSD_INNER:SSD_INNER + SSD_G * SSD_N]
        cm = xc[:, SSD_INNER + SSD_G * SSD_N:]
        dt = jax.nn.softplus(dt_ref[rows, :] + k.dtb[...])
        a_cs = _dot(k.lincl[...], jnp.concatenate(_split_terms(dt * neg_a, CUMSUM_TERMS), axis=0))
        a_last = a_cs[q - 1:q, :]
        a_cs_t = _dot_nt(k.eye[...], jnp.concatenate(_split_terms(a_cs, CUMSUM_TERMS), axis=1))
        cols = jnp.concatenate([dt, jnp.exp(a_cs), dt * jnp.exp(a_last - a_cs)], axis=0)
        cols_x = _dot(jnp.concatenate(_split_terms(cols, EXPAND_TERMS), axis=1), k.expand[...])
        dt_x, ea_x, dtdec_x = cols_x[0:q], cols_x[q:2 * q], cols_x[2 * q:3 * q]
        x_dt = (xs * dt_x).astype(BF16)
        x_dec = (xs * dtdec_x).astype(BF16)
        x_heads = [x_dt * k.hmask[r] for r in range(SSD_R)]
        yield
        for g in range(SSD_G):
            gl = slice(g * SSD_GW, (g + 1) * SSD_GW)
            b_g = bm[:, g * SSD_N:(g + 1) * SSD_N].astype(BF16)
            c_g = cm[:, g * SSD_N:(g + 1) * SSD_N].astype(BF16)
            cbm = jnp.where(causal, _dot_nt(c_g, b_g), 0.0)
            m_cat = jnp.concatenate(
                [(cbm * jnp.exp(jnp.minimum(a_cs[:, h:h + 1] - a_cs_t[h:h + 1, :], 0.0))).astype(BF16)
                 for h in range(g * SSD_R, (g + 1) * SSD_R)], axis=1)
            for r in range(SSD_R):
                xbd_scr[r * q:(r + 1) * q, :] = x_heads[r][:, gl]
            st = st_scr[g]
            y = _dot(m_cat, xbd_scr[...]) + _dot(c_g, st.astype(BF16)) * ea_x[:, gl] + xs[:, gl] * k.dsk[:, gl]
            st_scr[g] = st * ea_x[q - 1:q, gl] + _dot_tn(b_g, x_dec[:, gl])
            yg = y * _silu(z_ref[rows, gl].astype(F32))
            yg = yg * lax.rsqrt(jnp.mean(yg * yg, axis=-1, keepdims=True) + EPS)
            y_ref[rows, gl] = (yg * k.nw[:, gl]).astype(y_ref.dtype)
            yield
    xpad_scr[TAIL_ROWS - (SSD_CONV - 1):TAIL_ROWS, :] = xpad_scr[TAIL_ROWS + t - (SSD_CONV - 1):TAIL_ROWS + t, :]
    yield


N_SSD_CONSTS = len(_SsdConsts._fields)


def _ssd_body(z_ref, xbc_ref, dt_ref, *refs, q, n_sub):
    k = _SsdConsts(*refs[:N_SSD_CONSTS])
    st0_ref, tail0_ref, y_ref, stout_ref, tailout_ref, st_scr, xpad_scr, xbd_scr = refs[N_SSD_CONSTS:]
    c = pl.program_id(1)

    @pl.when(c == 0)
    def _():
        st_scr[...] = st0_ref[...]
        xpad_scr[0:TAIL_ROWS, :] = tail0_ref[...]

    _run(_ssd_stages(z_ref.at[0], xbc_ref.at[0], dt_ref.at[0], k, y_ref.at[0], st_scr, xpad_scr, xbd_scr,
                     q=q, n_sub=n_sub))

    @pl.when(c == pl.num_programs(1) - 1)
    def _():
        stout_ref[0] = st_scr[...]
        tailout_ref[0] = xpad_scr[0:TAIL_ROWS, :]


def _ssd_scratch(q, t):
    return [pltpu.VMEM((SSD_G, SSD_N, SSD_GW), F32),
            pltpu.VMEM((TAIL_ROWS + t, SSD_XBC), F32),
            pltpu.VMEM((SSD_R * q, SSD_GW), BF16)]


def _ssd_call(z, xbc, dt, consts, st0, tail0, q, n_sub):
    bsz, seq, _ = z.shape
    t = q * n_sub
    assert seq % t == 0 and t >= SSD_CONV - 1
    blk = lambda b, c: (b, c, 0)
    per_b3 = lambda b, c: (b, 0, 0)
    per_b4 = lambda b, c: (b, 0, 0, 0)
    return pl.pallas_call(
        functools.partial(_ssd_body, q=q, n_sub=n_sub),
        grid=(bsz, seq // t),
        in_specs=[
            pl.BlockSpec((1, t, SSD_INNER), blk),
            pl.BlockSpec((1, t, SSD_XBC), blk),
            pl.BlockSpec((1, t, LANES), blk),
        ] + [_const_spec(c.shape) for c in consts] + [_const_spec(st0.shape), _const_spec(tail0.shape)],
        out_specs=[
            pl.BlockSpec((1, t, SSD_INNER), blk),
            pl.BlockSpec((1, SSD_G, SSD_N, SSD_GW), per_b4),
            pl.BlockSpec((1, TAIL_ROWS, SSD_XBC), per_b3),
        ],
        out_shape=[
            jax.ShapeDtypeStruct((bsz, seq, SSD_INNER), BF16),
            jax.ShapeDtypeStruct((bsz, SSD_G, SSD_N, SSD_GW), F32),
            jax.ShapeDtypeStruct((bsz, TAIL_ROWS, SSD_XBC), F32),
        ],
        scratch_shapes=_ssd_scratch(q, t),
        compiler_params=_params(("arbitrary", "arbitrary")),
        name="ssd",
    )(z, xbc, dt, *consts, st0, tail0)


def _projssd_body(u_ref, wz_ref, wx_ref, wdt_ref, *refs, q, n_sub, blocks_per_seq):
    k = _SsdConsts(*refs[:N_SSD_CONSTS])
    st0_ref, tail0_ref, y_ref, z_scr, xbc_scr, dt_scr, st_scr, xpad_scr, xbd_scr = refs[N_SSD_CONSTS:]
    i = pl.program_id(0)

    @pl.when((i == 0) | ((i + blocks_per_seq - 1) % blocks_per_seq == 0))
    def _():
        st_scr[...] = st0_ref[...]
        xpad_scr[0:TAIL_ROWS, :] = tail0_ref[...]

    @pl.when(i == 0)
    def _():
        for scr in (z_scr, xbc_scr, dt_scr):
            scr[1] = jnp.zeros(scr.shape[1:], scr.dtype)

    slot = i % 2
    _run(_ssd_stages(z_scr.at[1 - slot], xbc_scr.at[1 - slot], dt_scr.at[1 - slot], k, y_ref,
                     st_scr, xpad_scr, xbd_scr, q=q, n_sub=n_sub),
         _proj_stages(u_ref, (wz_ref, wx_ref, wdt_ref), (z_scr.at[slot], xbc_scr.at[slot], dt_scr.at[slot])))


def _projssd_call(u2, wz, wx, wdt, consts, st0, tail0, bsz, seq, q, n_sub):
    t = q * n_sub
    assert seq % t == 0 and t >= SSD_CONV - 1
    blocks_per_seq = seq // t
    n_blocks = bsz * blocks_per_seq
    cur = lambda i: (jnp.minimum(i, n_blocks - 1), 0)
    prev = lambda i: (jnp.maximum(i - 1, 0), 0)
    fixed = (wz, wx, wdt) + tuple(consts) + (st0, tail0)
    return pl.pallas_call(
        functools.partial(_projssd_body, q=q, n_sub=n_sub, blocks_per_seq=blocks_per_seq),
        grid=(n_blocks + 1,),
        in_specs=[pl.BlockSpec((t, D_MODEL), cur)] + [_const_spec(c.shape) for c in fixed],
        out_specs=pl.BlockSpec((t, SSD_INNER), prev),
        out_shape=jax.ShapeDtypeStruct((bsz * seq, SSD_INNER), BF16),
        scratch_shapes=[pltpu.VMEM((2, t, SSD_INNER), BF16), pltpu.VMEM((2, t, SSD_XBC), BF16),
                        pltpu.VMEM((2, t, LANES), F32)] + _ssd_scratch(q, t),
        compiler_params=_params(("arbitrary",)),
        name="proj_ssd",
    )(u2, *fixed)


def _hg_levels(q):
    out, m = [], q // 2
    while m >= 1:
        out.append(m)
        m //= 2
    return out


SUBLANES = 8


def _hg_matmul_levels(q):
    return [m for m in _hg_levels(q) if 2 * m < SUBLANES]


def _hg_sum_matrices(q):
    t = np.arange(q)
    l, tt = t[:, None], t[None, :]
    mats = [tt <= l]
    for m in _hg_matmul_levels(q):
        ref = (l // (2 * m)) * (2 * m) + m - 1
        lower = (l % (2 * m)) >= m
        mats.append(np.where(lower, (tt > ref) & (tt <= l), (tt > l) & (tt <= ref)))
    return np.concatenate(mats, 0).astype(np.float32)


def _hg_level_log_decays(gc, sums, q):
    small = _hg_matmul_levels(q)
    w = gc.shape[-1]
    out = []
    for m in _hg_levels(q):
        if m in small:
            j = 1 + small.index(m)
            out.append(sums[j * q:(j + 1) * q])
        elif m >= SUBLANES:
            g4 = gc.reshape(q // (2 * m), 2, m, w)
            upper, lower = g4[:, 0], g4[:, 1]
            ref = upper[:, m - 1:m, :]
            out.append(jnp.stack([ref - upper, lower - ref], axis=1).reshape(q, w))
        else:
            g3 = gc.reshape(q // (2 * m), 2 * m, w)
            d = g3 - g3[:, m - 1:m, :]
            out.append(jnp.minimum(d, -d).reshape(q, w))
    return out


def _hg_level_masks(q):
    t = np.arange(q)
    l, s = t[:, None], t[None, :]
    masks = [(l > s) & (((l ^ s) >> int(np.log2(m))) == 1) for m in _hg_levels(q)] + [l == s]
    return np.stack([np.concatenate([mk, mk], 1) for mk in masks]).astype(np.float32)


def _block_diag_pair(x):
    z = jnp.zeros((x.shape[0], HG_K), x.dtype)
    return jnp.concatenate([jnp.concatenate([x[:, :HG_K], z], 1), jnp.concatenate([z, x[:, HG_K:]], 1)], 0)


def _hgrn2_stages(q_ref, f_ref, i_ref, g_ref, lbp_ref, nw_ref, sums_ref, masks_ref, y_ref, st_scr,
                  *, q, n_sub, layer):
    lbp = lbp_ref[...]
    e = jnp.exp(lbp - jnp.max(lbp, axis=0, keepdims=True))
    lb = jnp.sum(e[:layer + 1], axis=0, keepdims=True) / jnp.sum(e, axis=0, keepdims=True)

    n_lev = len(_hg_levels(q))

    def chunk(s):
        rows = slice(s * q, (s + 1) * q)
        f = lb + (1.0 - lb) * _sigmoid(f_ref[rows, :].astype(F32))
        kk_b = (1.0 - f).astype(BF16)
        qs_b = _silu(q_ref[rows, :].astype(F32)).astype(BF16)
        v = i_ref[rows, :]
        lf = jnp.log(f) * np.float32(np.log2(np.e))
        sums = _dot(sums_ref[...], jnp.concatenate(_split_terms(lf, EXPAND_TERMS), axis=0))
        gc = sums[0:q]
        p_inc = jnp.exp2(gc)
        p_last = p_inc[q - 1:q]
        p_lev = [jnp.exp2(e).astype(BF16) for e in _hg_level_log_decays(gc, sums, q)]
        wq = [qs_b * p for p in p_lev] + [qs_b]
        wk = [kk_b * p for p in p_lev] + [kk_b]
        qg = qs_b * p_inc.astype(BF16)
        kd = kk_b * jnp.exp2(gc[q - 1:q] - gc).astype(BF16)
        yield

        for pair in range(HG_HEADS // 2):
            pl2 = slice(2 * pair * HG_K, 2 * (pair + 1) * HG_K)
            att = jnp.zeros((q, 2 * q), BF16)
            for j in range(n_lev + 1):
                a_j = _dot_nt(wq[j][:, pl2], _block_diag_pair(wk[j][:, pl2]))
                att = att + a_j.astype(BF16) * masks_ref[j]
            st = st_scr[pair]
            o = _dot(att, _block_diag_pair(v[:, pl2])) + _dot_nt(qg[:, pl2], st.astype(BF16))
            for hh in range(2):
                h = 2 * pair + hh
                hl = slice(h * HG_K, (h + 1) * HG_K)
                bl = slice(hh * HG_K, (hh + 1) * HG_K)
                st_scr[pair, bl, bl] = st[bl, bl] * p_last[:, hl] + _dot_tn(v[:, hl], kd[:, hl])
                o_h = o[:, bl]
                o_h = o_h * lax.rsqrt(jnp.mean(o_h * o_h, axis=-1, keepdims=True) + EPS) * nw_ref[:, hl]
                y_ref[rows, hl] = (o_h * _silu(g_ref[rows, hl].astype(F32))).astype(y_ref.dtype)
            yield

    for s in range(n_sub):
        yield from chunk(s)


def _hgrn2_body(q_ref, f_ref, i_ref, g_ref, lbp_ref, nw_ref, sums_ref, masks_ref, st0_ref,
                y_ref, stout_ref, st_scr, *, q, n_sub, layer):
    c = pl.program_id(1)

    @pl.when(c == 0)
    def _():
        st_scr[...] = st0_ref[...]

    _run(_hgrn2_stages(q_ref.at[0], f_ref.at[0], i_ref.at[0], g_ref.at[0], lbp_ref, nw_ref, sums_ref, masks_ref,
                       y_ref.at[0], st_scr, q=q, n_sub=n_sub, layer=layer))

    @pl.when(c == pl.num_programs(1) - 1)
    def _():
        stout_ref[0] = st_scr[...]


def _hgrn2_call(qq, ff, ii, gg, lbp, nw, st0, q, n_sub, layer):
    bsz, seq, _ = qq.shape
    t = q * n_sub
    assert seq % t == 0 and q & (q - 1) == 0
    sums = jnp.asarray(_stack(_hg_sum_matrices(q), EXPAND_TERMS, 1), BF16)
    masks = jnp.asarray(_hg_level_masks(q), BF16)
    blk = lambda b, c: (b, c, 0)
    per_b4 = lambda b, c: (b, 0, 0, 0)
    tok = pl.BlockSpec((1, t, HG_WIDTH), blk)
    st_shape = (HG_HEADS // 2, 2 * HG_K, 2 * HG_K)
    return pl.pallas_call(
        functools.partial(_hgrn2_body, q=q, n_sub=n_sub, layer=layer),
        grid=(bsz, seq // t),
        in_specs=[tok, tok, tok, tok, _const_spec(lbp.shape), _const_spec(nw.shape),
                  _const_spec(sums.shape), _const_spec(masks.shape), _const_spec(st0.shape)],
        out_specs=[tok, pl.BlockSpec((1,) + st_shape, per_b4)],
        out_shape=[jax.ShapeDtypeStruct((bsz, seq, HG_WIDTH), BF16),
                   jax.ShapeDtypeStruct((bsz,) + st_shape, F32)],
        scratch_shapes=[pltpu.VMEM(st_shape, F32)],
        compiler_params=_params(("arbitrary", "arbitrary")),
        name="hgrn2",
    )(qq, ff, ii, gg, lbp, nw, sums, masks, st0)


def _tail_body(h_ref, ya_ref, yb_ref, ga_ref, gb_ref, wa_ref, wb_ref, wo_ref,
               nw_ref, wgu_ref, wd_ref, fnw_ref, o_ref, acc_ref):
    merged = (_sigmoid(ga_ref[...].astype(F32)) * _dot(ya_ref[...], wa_ref[...])
              + _sigmoid(gb_ref[...].astype(F32)) * _dot(yb_ref[...], wb_ref[...]))
    h = h_ref[...] + _dot(merged.astype(BF16), wo_ref[...])
    xn = _rms(h, nw_ref[...]).astype(BF16)
    _run(_ffn_stages(xn, wgu_ref, wd_ref, acc_ref))
    h = h + acc_ref[...]
    o_ref[...] = _rms(h, fnw_ref[...])


def _tail_call(h, ya, yb, ga, gb, wa, wb, wo, nw, wgu, wd, fnw, tm):
    m = h.shape[0]
    row = lambda i: (i, 0)
    tile = pl.BlockSpec((tm, D_MODEL), row)
    return pl.pallas_call(
        _tail_body,
        grid=(m // tm,),
        in_specs=[tile, tile, tile, tile, tile,
                  _const_spec(wa.shape), _const_spec(wb.shape), _const_spec(wo.shape),
                  _const_spec((1, D_MODEL)),
                  _const_spec(wgu.shape), _const_spec(wd.shape),
                  _const_spec((1, D_MODEL))],
        out_specs=tile,
        out_shape=jax.ShapeDtypeStruct((m, D_MODEL), F32),
        scratch_shapes=[pltpu.VMEM((tm, D_MODEL), F32)],
        compiler_params=_params(("arbitrary",)),
        name="tail",
    )(h, ya, yb, ga, gb, wa, wb, wo, nw, wgu, wd, fnw)


def _ffn_weights(w_gu, w_down):
    return w_gu.astype(BF16), (0.5 * w_down).astype(BF16)


def _pad_lanes(v):
    return jnp.zeros((1, LANES), F32).at[0, :v.shape[0]].set(v.astype(F32))


class _Tiling(NamedTuple):
    rows: int
    chunk: int
    ssd_sub: int
    hg_sub: int


MAIN_TILING = _Tiling(rows=512, chunk=128, ssd_sub=4, hg_sub=8)


def kernel(x, meta_tokens, ffn1_norm, ffn1_w_gu, ffn1_w_down, mix_norm, w_in, ssd_conv_w, ssd_conv_b,
           ssd_dt_bias, ssd_a_log, ssd_d, ssd_norm, hg_lower_bound, hg_norm, w_branch_a, w_branch_b,
           w_out, ffn2_norm, ffn2_w_gu, ffn2_w_down, final_norm):
    bsz, seq, d = x.shape
    assert d == D_MODEL and ffn1_norm.shape[0] == 1, "single-layer block of width D_MODEL only"
    n_meta = meta_tokens.shape[0]
    row = lambda v: v.reshape(1, -1).astype(F32)
    r3 = lambda t, b, s: t.reshape(b, s, t.shape[-1])

    wgu1, wd1 = _ffn_weights(ffn1_w_gu[0], ffn1_w_down[0])
    wgu2, wd2 = _ffn_weights(ffn2_w_gu[0], ffn2_w_down[0])
    sizes = (SSD_INNER, SSD_XBC, SSD_HEADS, HG_WIDTH, HG_WIDTH, HG_WIDTH, HG_WIDTH, D_MODEL, D_MODEL)
    offs = np.cumsum((0,) + sizes)
    w_t = jnp.swapaxes(w_in[0], 0, 1).astype(BF16)
    w_parts = [w_t[offs[j]:offs[j + 1], :] for j in range(len(sizes))]
    w_parts[2] = jnp.pad(w_parts[2], ((0, LANES - SSD_HEADS), (0, 0)))
    w_ssd, w_rest = w_parts[:3], w_parts[3:]
    ssd_dtypes, rest_dtypes = [BF16, BF16, F32], [BF16] * 6

    ssd_p = (ssd_conv_w[0].astype(F32), row(ssd_conv_b[0]), _pad_lanes(ssd_dt_bias[0]), _pad_lanes(ssd_a_log[0]),
             row(jnp.repeat(ssd_d[0], SSD_P)), row(ssd_norm[0]))
    hg_p = (hg_lower_bound.astype(F32), row(hg_norm[0]))

    _, u_m = _ffn1_call(meta_tokens.astype(F32), row(ffn1_norm[0]), wgu1, wd1, row(mix_norm[0]), n_meta)
    proj_m = _inproj_call(u_m, w_parts, ssd_dtypes + rest_dtypes, n_meta)
    zero_ssd = (jnp.zeros((SSD_G, SSD_N, SSD_GW), F32), jnp.zeros((TAIL_ROWS, SSD_XBC), F32))
    zero_hg = jnp.zeros((HG_HEADS // 2, 2 * HG_K, 2 * HG_K), F32)
    _, st_ssd, tail = _ssd_call(*(r3(t, 1, n_meta) for t in proj_m[:3]), _ssd_consts(*ssd_p, n_meta), *zero_ssd,
                                q=n_meta, n_sub=1)
    _, st_hg = _hgrn2_call(*(r3(t, 1, n_meta) for t in proj_m[3:7]), *hg_p, zero_hg, q=n_meta, n_sub=1, layer=0)

    tl = MAIN_TILING
    x2 = x.reshape(bsz * seq, d)
    h1, u = _ffn1_call(x2, row(ffn1_norm[0]), wgu1, wd1, row(mix_norm[0]), tl.rows)
    ya = _projssd_call(u, *w_ssd, _ssd_consts(*ssd_p, tl.chunk), st_ssd[0], tail[0], bsz, seq,
                       q=tl.chunk, n_sub=tl.ssd_sub)
    hq, hf, hi, hg, ga, gb = _inproj_call(u, w_rest, rest_dtypes, tl.rows)
    yb, _ = _hgrn2_call(*(r3(t, bsz, seq) for t in (hq, hf, hi, hg)), *hg_p, st_hg[0],
                        q=tl.chunk, n_sub=tl.hg_sub, layer=0)
    out = _tail_call(h1, ya, yb.reshape(bsz * seq, -1), ga, gb,
                     w_branch_a[0].astype(BF16), w_branch_b[0].astype(BF16), w_out[0].astype(BF16),
                     row(ffn2_norm[0]), wgu2, wd2, row(final_norm), tl.rows)
    return out.reshape(bsz, seq, d)
```
